```python
import math
import jax
import jax.numpy as jnp
from jax import lax
import numpy as np

D_MODEL = 1024
BATCH = 16
SEQ = 2048
DEPTH = 4

CTX_LEN = 256
GRID_W = 64
D_SSM = D_MODEL // 2
SSM_GROUP = 16
N_SSM_GROUPS = D_SSM // SSM_GROUP
SSM_STATE = 64
SSM_DT_MIN = 1e-3
SSM_DT_MAX = 1e-1
N_HEADS = 4
QK_HEAD_DIM = 64
V_HEAD_DIM = 2 * QK_HEAD_DIM
D_QK = N_HEADS * 2 * QK_HEAD_DIM
D_V = N_HEADS * V_HEAD_DIM
ROPE_BASE = 10000.0
ROPE_AXIS_FREQS = QK_HEAD_DIM // 4
Q_BLOCK = 128
IN_SPLITS = (D_SSM, D_SSM + D_QK, D_SSM + 2 * D_QK, D_SSM + 2 * D_QK + D_V, D_SSM + 2 * D_QK + D_V + D_MODEL)
IN_COLS = D_SSM + 2 * D_QK + D_V + 2 * D_MODEL
N_EXPERTS = 16
N_EXPERT_GROUPS = 4
EXPERTS_PER_GROUP = N_EXPERTS // N_EXPERT_GROUPS
TOP_K = 2
D_EXPERT = D_MODEL
MOE_BLOCK = 128
NORM_EPS = 1e-6
SUBLN_EPS = 1e-5

kernel_name = 'hybrid_s5_diffattn_groupmoe_dit'


def rmsnorm(x, g, eps=NORM_EPS):
    xf = x.astype(jnp.float32)
    y = xf * lax.rsqrt(jnp.mean(xf * xf, axis=-1, keepdims=True) + eps)
    return y.astype(x.dtype) * g


def axial_rope_tables(rows):
    row = jnp.repeat(jnp.arange(rows), GRID_W).astype(jnp.float32)
    col = jnp.tile(jnp.arange(GRID_W), rows).astype(jnp.float32)
    inv = ROPE_BASE ** (-jnp.arange(ROPE_AXIS_FREQS, dtype=jnp.float32) / ROPE_AXIS_FREQS)
    ang_r = row[:, None] * inv
    ang_c = col[:, None] * inv
    ang = jnp.concatenate([ang_r, ang_r, ang_c, ang_c], axis=-1)
    return jnp.cos(ang), jnp.sin(ang)


def apply_axial_rope(t, cos, sin):
    shp = t.shape
    tt = t.reshape(shp[:-1] + (2, 2, ROPE_AXIS_FREQS))
    rot = jnp.stack([-tt[..., 1, :], tt[..., 0, :]], axis=-2).reshape(shp)
    cs = cos[None, :, None, None, :].astype(t.dtype)
    sn = sin[None, :, None, None, :].astype(t.dtype)
    return t * cs + rot * sn


def s5_discretize(lam_re, lam_im, log_dt, b_re, b_im):
    lr = lam_re.astype(jnp.float32)
    li = lam_im.astype(jnp.float32)
    dt = jnp.exp(log_dt.astype(jnp.float32))[:, None]
    decay = jnp.exp(lr * dt)
    ab_re = decay * jnp.cos(li * dt)
    ab_im = decay * jnp.sin(li * dt)
    den = lr * lr + li * li
    nr = ab_re - 1.0
    ni = ab_im
    cr = ((nr * lr + ni * li) / den)[..., None]
    ci = ((ni * lr - nr * li) / den)[..., None]
    br = b_re.astype(jnp.float32)
    bi = b_im.astype(jnp.float32)
    return ab_re, ab_im, cr * br - ci * bi, cr * bi + ci * br


def _linear_recurrence_combine(left, right):
    a1r, a1i, b1r, b1i = left
    a2r, a2i, b2r, b2i = right
    return (a2r * a1r - a2i * a1i,
            a2r * a1i + a2i * a1r,
            a2r * b1r - a2i * b1i + b2r,
            a2r * b1i + a2i * b1r + b2i)


def s5_scan(u_g, ab_re, ab_im, bb_re, bb_im, h0, reverse):
    uf = u_g.astype(jnp.float32)
    bu_re = jnp.einsum('blgh,gph->blgp', uf, bb_re)
    bu_im = jnp.einsum('blgh,gph->blgp', uf, bb_im)
    if h0 is not None:
        pos = u_g.shape[1] - 1 if reverse else 0
        h0_re, h0_im = h0
        bu_re = bu_re.at[:, pos].add(ab_re * h0_re - ab_im * h0_im)
        bu_im = bu_im.at[:, pos].add(ab_re * h0_im + ab_im * h0_re)
    shape = (1, u_g.shape[1]) + ab_re.shape
    a_re = jnp.broadcast_to(ab_re, shape)
    a_im = jnp.broadcast_to(ab_im, shape)
    _, _, h_re, h_im = lax.associative_scan(_linear_recurrence_combine, (a_re, a_im, bu_re, bu_im),
                                            reverse=reverse, axis=1)
    return h_re, h_im


def s5_readout(h_re, h_im, c_re, c_im):
    return (jnp.einsum('blgp,ghp->blgh', h_re, c_re.astype(jnp.float32))
            - jnp.einsum('blgp,ghp->blgh', h_im, c_im.astype(jnp.float32)))


def s5_glu(y, w_glu):
    a, b = jnp.split(jax.nn.gelu(y) @ w_glu, 2, axis=-1)
    return a * jax.nn.sigmoid(b)


def s5_branch(u, uc, lam_re, lam_im, log_dt, b_re, b_im, c_re, c_im, d_skip, w_glu, ctx_out):
    B, S, _ = u.shape
    Lc = uc.shape[1]
    u_g = u.reshape(B, S, N_SSM_GROUPS, SSM_GROUP)
    uc_g = uc.reshape(B, Lc, N_SSM_GROUPS, SSM_GROUP)
    y = d_skip * u
    yc = d_skip * uc if ctx_out else None
    for dirn, reverse in ((0, False), (1, True)):
        ab_re, ab_im, bb_re, bb_im = s5_discretize(lam_re[dirn], lam_im[dirn], log_dt[dirn], b_re[dirn], b_im[dirn])
        hc_re, hc_im = s5_scan(uc_g, ab_re, ab_im, bb_re, bb_im, None, reverse)
        last = 0 if reverse else Lc - 1
        h_re, h_im = s5_scan(u_g, ab_re, ab_im, bb_re, bb_im, (hc_re[:, last], hc_im[:, last]), reverse)
        y = y + s5_readout(h_re, h_im, c_re[dirn], c_im[dirn]).reshape(B, S, D_SSM).astype(u.dtype)
        if ctx_out:
            yc = yc + s5_readout(hc_re, hc_im, c_re[dirn], c_im[dirn]).reshape(B, Lc, D_SSM).astype(u.dtype)
    return s5_glu(y, w_glu), (s5_glu(yc, w_glu) if ctx_out else None)


def diff_attn_core(q, k, v, lam):
    s = jnp.einsum('bqhmd,bkhmd->bhmqk', q, k).astype(jnp.float32) * (QK_HEAD_DIM ** -0.5)
    p = jax.nn.softmax(s, axis=-1)
    a = p[:, :, 0] - lam * p[:, :, 1]
    return jnp.einsum('bhqk,bkhd->bqhd', a.astype(v.dtype), v)


def diff_attention_branch(q, k, v, qc, kc, vc, lam_vec, lam_init, subln_g, w_attn_out, cos, sin, ctx_out):
    B, S, _ = q.shape
    Lc = kc.shape[1]
    q = apply_axial_rope(q.reshape(B, S, N_HEADS, 2, QK_HEAD_DIM), cos, sin)
    k = apply_axial_rope(k.reshape(B, S, N_HEADS, 2, QK_HEAD_DIM), cos, sin)
    v = v.reshape(B, S, N_HEADS, V_HEAD_DIM)
    kc = kc.reshape(B, Lc, N_HEADS, 2, QK_HEAD_DIM)
    vc = vc.reshape(B, Lc, N_HEADS, V_HEAD_DIM)
    lf = lam_vec.astype(jnp.float32)
    lam = jnp.exp(jnp.sum(lf[0] * lf[1])) - jnp.exp(jnp.sum(lf[2] * lf[3])) + lam_init
    k_all = jnp.concatenate([kc, k], axis=1)
    v_all = jnp.concatenate([vc, v], axis=1)
    n_blk = S // Q_BLOCK
    qb = jnp.moveaxis(q.reshape(B, n_blk, Q_BLOCK, N_HEADS, 2, QK_HEAD_DIM), 1, 0)
    ob = lax.map(lambda qi: diff_attn_core(qi, k_all, v_all, lam), qb)
    o = jnp.moveaxis(ob, 0, 1).reshape(B, S, N_HEADS, V_HEAD_DIM)

    def post(t):
        t = rmsnorm(t, subln_g, SUBLN_EPS) * (1.0 - lam_init)
        return t.reshape(t.shape[0], t.shape[1], D_V) @ w_attn_out

    out = post(o)
    if ctx_out:
        oc = diff_attn_core(qc.reshape(B, Lc, N_HEADS, 2, QK_HEAD_DIM), kc, vc, lam)
        return out, post(oc)
    return out, None


def grouped_experts(t, eidx, gate_w, w1, w3, w2):
    T, D = t.shape
    n_assign = T * TOP_K
    e_flat = eidx.reshape(-1)
    tok_flat = jnp.arange(n_assign) // TOP_K
    w_flat = gate_w.reshape(-1)
    order = jnp.argsort(e_flat)
    e_sorted = e_flat[order]
    counts = jnp.bincount(e_flat, length=N_EXPERTS)
    padded = (counts + MOE_BLOCK - 1) // MOE_BLOCK * MOE_BLOCK
    start = jnp.cumsum(counts) - counts
    pend = jnp.cumsum(padded)
    pstart = pend - padded
    dest = pstart[e_sorted] + (jnp.arange(n_assign) - start[e_sorted])
    n_blocks = (n_assign + N_EXPERTS * (MOE_BLOCK - 1) + MOE_BLOCK - 1) // MOE_BLOCK
    cap = n_blocks * MOE_BLOCK
    slot_tok = jnp.full((cap,), T, jnp.int32).at[dest].set(tok_flat[order].astype(jnp.int32))
    slot_w = jnp.zeros((cap,), t.dtype).at[dest].set(w_flat[order].astype(t.dtype))
    block_exp = jnp.minimum(jnp.searchsorted(pend, jnp.arange(n_blocks) * MOE_BLOCK, side='right'), N_EXPERTS - 1)
    t_pad = jnp.concatenate([t, jnp.zeros((1, D), t.dtype)], axis=0)
    xb = t_pad[slot_tok].reshape(n_blocks, MOE_BLOCK, D)

    def expert_block(args):
        xe, e = args
        hid = jax.nn.silu(xe @ w1[e]) * (xe @ w3[e])
        return hid @ w2[e]

    yb = lax.map(expert_block, (xb, block_exp)).reshape(cap, D)
    out = jnp.zeros((T + 1, D), t.dtype).at[slot_tok].add(yb * slot_w[:, None])
    return out[:T]


def moe_ffn(t, router_w, router_b, w1, w3, w2):
    T = t.shape[0]
    logits = t.astype(jnp.float32) @ router_w.astype(jnp.float32)
    aff = jax.nn.sigmoid(logits)
    sel = aff + router_b.astype(jnp.float32)
    group_score = lax.top_k(sel.reshape(T, N_EXPERT_GROUPS, EXPERTS_PER_GROUP), 2)[0].sum(-1)
    g_sel = jnp.argmax(group_score, axis=-1)
    in_group = (jnp.arange(N_EXPERTS) // EXPERTS_PER_GROUP)[None, :] == g_sel[:, None]
    _, eidx = lax.top_k(jnp.where(in_group, sel, -jnp.inf), TOP_K)
    w = jnp.take_along_axis(aff, eidx, axis=-1)
    w = w / jnp.sum(w, axis=-1, keepdims=True)
    return grouped_experts(t, eidx, w, w1, w3, w2)


def setup_inputs(seed: int = 0) -> dict:
    key = jax.random.key(seed)
    ks = jax.random.split(key, 32)
    f32 = jnp.float32

    def nrm(k, shape, s):
        return jax.random.normal(k, shape, f32) * s

    lam_shape = (DEPTH, 2, N_SSM_GROUPS, SSM_STATE)
    n_idx = jnp.arange(SSM_STATE, dtype=f32)
    return {
        'x': nrm(ks[0], (BATCH, SEQ, D_MODEL), 1.0),
        'c': nrm(ks[1], (BATCH, D_MODEL), 1.0),
        'ctx': nrm(ks[2], (BATCH, CTX_LEN, D_MODEL), 1.0),
        'c_ctx': nrm(ks[3], (D_MODEL,), 1.0),
        'ada_w': nrm(ks[4], (DEPTH, D_MODEL, 6 * D_MODEL), 0.5 * D_MODEL ** -0.5),
        'ada_b': nrm(ks[5], (DEPTH, 6 * D_MODEL), 0.02),
        'norm1_g': 1.0 + nrm(ks[6], (DEPTH, D_MODEL), 0.02),
        'norm2_g': 1.0 + nrm(ks[7], (DEPTH, D_MODEL), 0.02),
        'final_g': 1.0 + nrm(ks[8], (D_MODEL,), 0.02),
        'w_in': nrm(ks[9], (DEPTH, D_MODEL, IN_COLS), D_MODEL ** -0.5),
        'ssm_lam_re': -0.5 + nrm(ks[10], lam_shape, 0.01),
        'ssm_lam_im': math.pi * n_idx + nrm(ks[11], lam_shape, 0.01),
        'ssm_log_dt': jax.random.uniform(ks[12], (DEPTH, 2, N_SSM_GROUPS), f32,
                                         math.log(SSM_DT_MIN), math.log(SSM_DT_MAX)),
        'ssm_b_re': nrm(ks[13], (DEPTH, 2, N_SSM_GROUPS, SSM_STATE, SSM_GROUP), (2 * SSM_GROUP) ** -0.5),
        'ssm_b_im': nrm(ks[14], (DEPTH, 2, N_SSM_GROUPS, SSM_STATE, SSM_GROUP), (2 * SSM_GROUP) ** -0.5),
        'ssm_c_re': nrm(ks[15], (DEPTH, 2, N_SSM_GROUPS, SSM_GROUP, SSM_STATE), SSM_STATE ** -0.5),
        'ssm_c_im': nrm(ks[16], (DEPTH, 2, N_SSM_GROUPS, SSM_GROUP, SSM_STATE), SSM_STATE ** -0.5),
        'ssm_d': nrm(ks[17], (DEPTH, D_SSM), 1.0),
        'w_glu': nrm(ks[18], (DEPTH, D_SSM, 2 * D_MODEL), D_SSM ** -0.5),
        'attn_lambda': nrm(ks[19], (DEPTH, 4, QK_HEAD_DIM), 0.1),
        'attn_subln_g': 1.0 + nrm(ks[20], (DEPTH, V_HEAD_DIM), 0.02),
        'w_attn_out': nrm(ks[21], (DEPTH, D_V, D_MODEL), D_V ** -0.5),
        'w_o': nrm(ks[22], (DEPTH, D_MODEL, D_MODEL), D_MODEL ** -0.5),
        'router_w': nrm(ks[23], (D_MODEL, N_EXPERTS), D_MODEL ** -0.5),
        'router_b': nrm(ks[24], (N_EXPERTS,), 0.01),
        'exp_w1': nrm(ks[25], (DEPTH, N_EXPERTS, D_MODEL, D_EXPERT), D_MODEL ** -0.5),
        'exp_w3': nrm(ks[26], (DEPTH, N_EXPERTS, D_MODEL, D_EXPERT), D_MODEL ** -0.5),
        'exp_w2': nrm(ks[27], (DEPTH, N_EXPERTS, D_EXPERT, D_MODEL), D_EXPERT ** -0.5),
    }


def reference(x, c, ctx, c_ctx, ada_w, ada_b, norm1_g, norm2_g, final_g, w_in,
              ssm_lam_re, ssm_lam_im, ssm_log_dt, ssm_b_re, ssm_b_im, ssm_c_re, ssm_c_im, ssm_d, w_glu,
              attn_lambda, attn_subln_g, w_attn_out, w_o, router_w, router_b, exp_w1, exp_w3, exp_w2):
    B, S, D = x.shape
    Lc = ctx.shape[1]
    rows = S // GRID_W
    cos, sin = axial_rope_tables(rows)
    silu_c = jax.nn.silu(c)
    silu_cc = jax.nn.silu(c_ctx)
    xc = ctx
    for l in range(DEPTH):
        ctx_out = l < DEPTH - 1
        lam_init = 0.8 - 0.6 * math.exp(-0.3 * l)
        sh1, sc1, g1, sh2, sc2, g2 = [m[:, None, :] for m in jnp.split(silu_c @ ada_w[l] + ada_b[l], 6, axis=-1)]
        csh1, csc1, cg1, csh2, csc2, cg2 = jnp.split(silu_cc @ ada_w[l] + ada_b[l], 6, axis=-1)

        h = rmsnorm(x, norm1_g[l]) * (1.0 + sc1) + sh1
        hc = rmsnorm(xc, norm1_g[l]) * (1.0 + csc1) + csh1
        u, q, k, v, gs, ga = jnp.split(h @ w_in[l], IN_SPLITS, axis=-1)
        uc, qc, kc, vc, gsc, gac = jnp.split(hc @ w_in[l], IN_SPLITS, axis=-1)
        p_s, p_sc = s5_branch(u, uc, ssm_lam_re[l], ssm_lam_im[l], ssm_log_dt[l], ssm_b_re[l], ssm_b_im[l],
                              ssm_c_re[l], ssm_c_im[l], ssm_d[l], w_glu[l], ctx_out)
        p_a, p_ac = diff_attention_branch(q, k, v, qc, kc, vc, attn_lambda[l], lam_init, attn_subln_g[l],
                                          w_attn_out[l], cos, sin, ctx_out)
        x = x + g1 * ((jax.nn.sigmoid(gs) * p_s + jax.nn.sigmoid(ga) * p_a) @ w_o[l])

        h2 = rmsnorm(x, norm2_g[l]) * (1.0 + sc2) + sh2
        if ctx_out:
            xc = xc + cg1 * ((jax.nn.sigmoid(gsc) * p_sc + jax.nn.sigmoid(gac) * p_ac) @ w_o[l])
            h2c = rmsnorm(xc, norm2_g[l]) * (1.0 + csc2) + csh2
            tokens = jnp.concatenate([h2.reshape(B * S, D), h2c.reshape(B * Lc, D)], axis=0)
            f = moe_ffn(tokens, router_w, router_b, exp_w1[l], exp_w3[l], exp_w2[l])
            x = x + g2 * f[:B * S].reshape(B, S, D)
            xc = xc + cg2 * f[B * S:].reshape(B, Lc, D)
        else:
            f = moe_ffn(h2.reshape(B * S, D), router_w, router_b, exp_w1[l], exp_w3[l], exp_w2[l])
            x = x + g2 * f.reshape(B, S, D)
    return rmsnorm(x, final_g)
```

```python
import functools
import math

import jax
import jax.numpy as jnp
from jax import lax
from jax.experimental import pallas as pl
from jax.experimental.pallas import tpu as pltpu

F32 = jnp.float32
BF16 = jnp.bfloat16
HIGHEST = lax.Precision.HIGHEST

GRID_W = 64
SSM_GROUP = 16
SSM_STATE = 64
N_HEADS = 4
QK_HEAD_DIM = 64
V_HEAD_DIM = 128
ROPE_BASE = 10000.0
ROPE_AXIS_FREQS = QK_HEAD_DIM // 4
N_EXPERTS = 16
EXPERTS_PER_GROUP = 4
N_EXPERT_GROUPS = N_EXPERTS // EXPERTS_PER_GROUP
NORM_EPS = 1e-6
SUBLN_EPS = 1e-5

PAIRS = ((0, 1), (0, 2), (0, 3), (1, 2), (1, 3), (2, 3))
N_CLASSES = N_EXPERT_GROUPS * len(PAIRS)

TOKEN_TILE = 256
SSM_CHUNK = 16
EXPERT_BLOCK = 256
VMEM_LIMIT = 56 * 1024 * 1024


def _dot(a, b):
    return jnp.dot(a, b, preferred_element_type=F32)


def _cparams(sem):
    return pltpu.CompilerParams(dimension_semantics=sem, vmem_limit_bytes=VMEM_LIMIT)


def _ada_kernel(c_ref, w_ref, b_ref, o_ref):
    c = c_ref[...]
    s = c * jax.nn.sigmoid(c)
    o_ref[0] = jnp.dot(s, w_ref[0], precision=HIGHEST, preferred_element_type=F32) + b_ref[0]


def ada_modulation(c_all, ada_w, ada_b):
    depth, d, n = ada_w.shape
    r = c_all.shape[0]
    tn = n // 4
    return pl.pallas_call(
        _ada_kernel,
        grid=(depth, n // tn),
        in_specs=[
            pl.BlockSpec((r, d), lambda l, j: (0, 0)),
            pl.BlockSpec((1, d, tn), lambda l, j: (l, 0, j)),
            pl.BlockSpec((1, 1, tn), lambda l, j: (l, 0, j)),
        ],
        out_specs=pl.BlockSpec((1, r, tn), lambda l, j: (l, 0, j)),
        out_shape=jax.ShapeDtypeStruct((depth, r, n), F32),
        compiler_params=_cparams(("arbitrary", "arbitrary")),
        name="ada_modulation",
    )(c_all, ada_w, ada_b.reshape(depth, 1, n))


def _inproj_kernel(x_ref, mod_ref, g_ref, w_ref, cos_ref, sa_ref, sb_ref,
                   u_ref, q_ref, k_ref, v_ref, gate_ref, *, d_ssm, d_qk, d_v):
    x = x_ref[0]
    m = mod_ref[0]
    ms = jnp.mean(x * x, axis=-1, keepdims=True)
    y = x * lax.rsqrt(ms + NORM_EPS)
    h = (y * g_ref[...]) * (1.0 + m[1:2]) + m[0:1]
    hb = h.astype(BF16)

    cos = cos_ref[...]
    sa = sa_ref[...]
    sb = sb_ref[...]
    half = ROPE_AXIS_FREQS

    def rope(t):
        return t * cos + pltpu.roll(t, d_qk - half, 1) * sa + pltpu.roll(t, half, 1) * sb

    o0 = 0
    u_ref[0] = _dot(hb, w_ref[:, o0:o0 + d_ssm]).astype(BF16)
    o0 += d_ssm
    q_ref[0] = (rope(_dot(hb, w_ref[:, o0:o0 + d_qk])) * (QK_HEAD_DIM ** -0.5)).astype(BF16)
    o0 += d_qk
    k_ref[0] = rope(_dot(hb, w_ref[:, o0:o0 + d_qk])).astype(BF16)
    o0 += d_qk
    v_ref[0] = _dot(hb, w_ref[:, o0:o0 + d_v]).astype(BF16)
    o0 += d_v
    n_gate = gate_ref.shape[-1]
    gate_ref[0] = jax.nn.sigmoid(_dot(hb, w_ref[:, o0:o0 + n_gate])).astype(BF16)


def in_projection(xx, mods, g, w_bf, cos_t, sa_t, sb_t, n_ctx_tiles):
    b, lt, d = xx.shape
    nj = lt // TOKEN_TILE
    n_lat = b
    d_ssm = d // 2
    d_qk = N_HEADS * 2 * QK_HEAD_DIM
    d_v = N_HEADS * V_HEAD_DIM
    n_gate = 2 * d
    tm = TOKEN_TILE

    def mod_idx(i, j):
        return (jnp.where(j < n_ctx_tiles, n_lat, i), 0, 0)

    tok = lambda i, j: (i, j, 0)
    kern = functools.partial(_inproj_kernel, d_ssm=d_ssm, d_qk=d_qk, d_v=d_v)
    return pl.pallas_call(
        kern,
        grid=(b, nj),
        in_specs=[
            pl.BlockSpec((1, tm, d), tok),
            pl.BlockSpec((1, 6, d), mod_idx),
            pl.BlockSpec((1, d), lambda i, j: (0, 0)),
            pl.BlockSpec(w_bf.shape, lambda i, j: (0, 0)),
            pl.BlockSpec((tm, d_qk), lambda i, j: (j, 0)),
            pl.BlockSpec((tm, d_qk), lambda i, j: (j, 0)),
            pl.BlockSpec((tm, d_qk), lambda i, j: (j, 0)),
        ],
        out_specs=[
            pl.BlockSpec((1, tm, d_ssm), tok),
            pl.BlockSpec((1, tm, d_qk), tok),
            pl.BlockSpec((1, tm, d_qk), tok),
            pl.BlockSpec((1, tm, d_v), tok),
            pl.BlockSpec((1, tm, n_gate), tok),
        ],
        out_shape=[
            jax.ShapeDtypeStruct((b, lt, d_ssm), BF16),
            jax.ShapeDtypeStruct((b, lt, d_qk), BF16),
            jax.ShapeDtypeStruct((b, lt, d_qk), BF16),
            jax.ShapeDtypeStruct((b, lt, d_v), BF16),
            jax.ShapeDtypeStruct((b, lt, n_gate), BF16),
        ],
        compiler_params=_cparams(("arbitrary", "arbitrary")),
        name="in_projection",
    )(xx, mods, g.reshape(1, d), w_bf, cos_t, sa_t, sb_t)


def rope_tables(lc, s):
    rows = s // GRID_W
    row = jnp.repeat(jnp.arange(rows), GRID_W).astype(F32)
    col = jnp.tile(jnp.arange(GRID_W), rows).astype(F32)
    inv = ROPE_BASE ** (-jnp.arange(ROPE_AXIS_FREQS, dtype=F32) / ROPE_AXIS_FREQS)
    ang_r = row[:, None] * inv
    ang_c = col[:, None] * inv
    ang = jnp.concatenate([ang_r, ang_r, ang_c, ang_c], axis=-1)
    cos = jnp.concatenate([jnp.ones((lc, QK_HEAD_DIM), F32), jnp.cos(ang)], axis=0)
    sin = jnp.concatenate([jnp.zeros((lc, QK_HEAD_DIM), F32), jnp.sin(ang)], axis=0)
    first_half = (jnp.arange(QK_HEAD_DIM) % (2 * ROPE_AXIS_FREQS)) < ROPE_AXIS_FREQS
    sa = jnp.where(first_half, -sin, 0.0)
    sb = jnp.where(first_half, 0.0, sin)
    reps = N_HEADS * 2
    return jnp.tile(cos, (1, reps)), jnp.tile(sa, (1, reps)), jnp.tile(sb, (1, reps))


def _cmul(ar, ai, br, bi):
    return ar * br - ai * bi, ar * bi + ai * br


def s5_chunk_weights(lam_re, lam_im, log_dt, b_re, b_im, c_re, c_im, d_skip):
    c = SSM_CHUNK
    lr = lam_re.astype(F32)
    li = lam_im.astype(F32)
    dt = jnp.exp(log_dt.astype(F32))[..., None]
    decay = jnp.exp(lr * dt)
    ab_re = decay * jnp.cos(li * dt)
    ab_im = decay * jnp.sin(li * dt)
    den = lr * lr + li * li
    nr = ab_re - 1.0
    ni = ab_im
    cr = ((nr * lr + ni * li) / den)[..., None]
    ci = ((ni * lr - nr * li) / den)[..., None]
    br = b_re.astype(F32)
    bi = b_im.astype(F32)
    bb_re = cr * br - ci * bi
    bb_im = cr * bi + ci * br
    ccr = c_re.astype(F32)
    cci = c_im.astype(F32)

    pr = [jnp.ones_like(ab_re)]
    pi = [jnp.zeros_like(ab_im)]
    for _ in range(c):
        r, i = _cmul(pr[-1], pi[-1], ab_re, ab_im)
        pr.append(r)
        pi.append(i)
    pw_re = jnp.stack(pr)
    pw_im = jnp.stack(pi)

    pb_re, pb_im = _cmul(pw_re[..., None], pw_im[..., None], bb_re[None], bb_im[None])
    cp_re, cp_im = _cmul(ccr[None], cci[None], pw_re[:, :, :, None, :], pw_im[:, :, :, None, :])
    kk = (jnp.einsum('dgop,tdgpi->tdgoi', ccr, pb_re[:c], precision=HIGHEST)
          - jnp.einsum('dgop,tdgpi->tdgoi', cci, pb_im[:c], precision=HIGHEST))

    g = lr.shape[1]
    h = SSM_GROUP
    p = SSM_STATE
    ii = jnp.arange(c)[:, None]
    jj = jnp.arange(c)[None, :]
    lag_f = jnp.clip(jj - ii, 0, c - 1)
    lag_b = jnp.clip(ii - jj, 0, c - 1)
    kf = kk[:, 0]
    kb = kk[:, 1]
    tf = jnp.where((jj >= ii)[:, :, None, None, None], kf[lag_f], 0.0)
    tb = jnp.where((ii >= jj)[:, :, None, None, None], kb[lag_b], 0.0)
    tt = jnp.transpose(tf + tb, (2, 0, 4, 1, 3))
    dd = d_skip.astype(F32).reshape(g, h)
    eye_t = jnp.eye(c, dtype=F32)
    eye_h = jnp.eye(h, dtype=F32)
    tt = tt + (eye_t[None, :, None, :, None] * eye_h[None, None, :, None, :]
               * dd[:, None, :, None, None])
    tt = tt.reshape(g, c * h, c * h)

    def s_cols(pbx, d, rev):
        m = pbx[:c, d]
        if rev:
            m = m[::-1]
        return jnp.transpose(m, (1, 0, 3, 2)).reshape(g, c * h, p)

    w1 = jnp.concatenate([
        tt,
        s_cols(pb_re, 0, True), s_cols(pb_re, 1, False),
        s_cols(pb_im, 0, True), s_cols(pb_im, 1, False)], axis=-1)

    def o_rows(cpx, d, sign, fwd):
        m = cpx[1:c + 1, d]
        if not fwd:
            m = m[::-1]
        return sign * jnp.transpose(m, (1, 3, 0, 2)).reshape(g, p, c * h)

    w2 = jnp.concatenate([
        o_rows(cp_re, 0, 1.0, True), o_rows(cp_re, 1, 1.0, False),
        o_rows(cp_im, 0, -1.0, True), o_rows(cp_im, 1, -1.0, False)], axis=1)

    a16 = jnp.zeros((g, 8, 2 * p), F32)
    a16 = a16.at[:, 0, :].set(jnp.concatenate([pw_re[c, 0], pw_re[c, 1]], axis=-1))
    a16 = a16.at[:, 1, :].set(jnp.concatenate([pw_im[c, 0], pw_im[c, 1]], axis=-1))
    return w1.astype(BF16), w2.astype(BF16), a16


def _s5_kernel(x_ref, w1_ref, w2_ref, a_ref, y_ref, z_ref, hin_ref, *, nb, nc_ctx, nc_lat):
    ch = x_ref.shape[-1]
    p = SSM_STATE
    z_ref[...] = _dot(x_ref[0], w1_ref[0])
    a_re = a_ref[0, 0:1, :]
    a_im = a_ref[0, 1:2, :]
    fwd_lane = lax.broadcasted_iota(jnp.int32, (nb, 2 * p), 1) < p

    def phase(base, n, carry):
        def step(i, hc):
            h_re, h_im = hc
            rf = pl.multiple_of((base + i) * nb, nb)
            rb = pl.multiple_of((base + n - 1 - i) * nb, nb)
            hin_ref[pl.ds(rf, nb), 0:p] = h_re[:, 0:p]
            hin_ref[pl.ds(rb, nb), p:2 * p] = h_re[:, p:2 * p]
            hin_ref[pl.ds(rf, nb), 2 * p:3 * p] = h_im[:, 0:p]
            hin_ref[pl.ds(rb, nb), 3 * p:4 * p] = h_im[:, p:2 * p]
            s_re = jnp.where(fwd_lane, z_ref[pl.ds(rf, nb), ch:ch + 2 * p],
                             z_ref[pl.ds(rb, nb), ch:ch + 2 * p])
            s_im = jnp.where(fwd_lane, z_ref[pl.ds(rf, nb), ch + 2 * p:ch + 4 * p],
                             z_ref[pl.ds(rb, nb), ch + 2 * p:ch + 4 * p])
            n_re = a_re * h_re - a_im * h_im + s_re
            n_im = a_re * h_im + a_im * h_re + s_im
            return n_re, n_im
        return lax.fori_loop(0, n, step, carry)

    zero = jnp.zeros((nb, 2 * p), F32)
    hc = phase(0, nc_ctx, (zero, zero))
    phase(nc_ctx, nc_lat, hc)
    y = z_ref[:, 0:ch] + _dot(hin_ref[...].astype(BF16), w2_ref[0])
    y_ref[0] = jax.nn.gelu(y).astype(BF16)


def s5_mixer(u, w1, w2, a16, lc):
    b, lt, dssm = u.shape
    c = SSM_CHUNK
    h = SSM_GROUP
    g = dssm // h
    nc = lt // c
    m = nc * b
    ug = u.reshape(b, nc, c, g, h).transpose(3, 1, 0, 2, 4).reshape(g, m, c * h)
    kern = functools.partial(_s5_kernel, nb=b, nc_ctx=lc // c, nc_lat=(lt - lc) // c)
    yg = pl.pallas_call(
        kern,
        grid=(g,),
        in_specs=[
            pl.BlockSpec((1, m, c * h), lambda i: (i, 0, 0)),
            pl.BlockSpec((1,) + w1.shape[1:], lambda i: (i, 0, 0)),
            pl.BlockSpec((1,) + w2.shape[1:], lambda i: (i, 0, 0)),
            pl.BlockSpec((1, 8, 2 * SSM_STATE), lambda i: (i, 0, 0)),
        ],
        out_specs=pl.BlockSpec((1, m, c * h), lambda i: (i, 0, 0)),
        out_shape=jax.ShapeDtypeStruct((g, m, c * h), BF16),
        scratch_shapes=[pltpu.VMEM((m, w1.shape[-1]), F32), pltpu.VMEM((m, 4 * SSM_STATE), F32)],
        compiler_params=_cparams(("arbitrary",)),
        name="s5_mixer",
    )(ug, w1, w2, a16)
    return yg.reshape(g, nc, b, c, h).transpose(2, 1, 3, 0, 4).reshape(b, lt, dssm)


def _attn_kernel(lam_ref, q_ref, k_ref, v_ref, g_ref, o_ref, *, n_ctx_tiles, lc, out_scale):
    j = pl.program_id(2)
    lam = lam_ref[0]
    q = q_ref[0]
    lo = lax.broadcasted_iota(jnp.int32, q.shape, 1) < QK_HEAD_DIM
    zq = jnp.zeros_like(q)
    q1 = jnp.where(lo, q, zq)
    q2 = jnp.where(lo, zq, q)
    nt = (((1,), (1,)), ((), ()))

    def run(nk):
        k = k_ref[0, 0:nk, :]
        v = v_ref[0, 0:nk, :]
        s1 = lax.dot_general(q1, k, nt, preferred_element_type=F32)
        s2 = lax.dot_general(q2, k, nt, preferred_element_type=F32)
        e1 = jnp.exp(s1 - jnp.max(s1, axis=-1, keepdims=True))
        e2 = jnp.exp(s2 - jnp.max(s2, axis=-1, keepdims=True))
        r1 = 1.0 / jnp.sum(e1, axis=-1, keepdims=True)
        r2 = lam / jnp.sum(e2, axis=-1, keepdims=True)
        a = e1 * r1 - e2 * r2
        o = _dot(a.astype(BF16), v)
        ms = jnp.mean(o * o, axis=-1, keepdims=True)
        o = (o * lax.rsqrt(ms + SUBLN_EPS)) * g_ref[...] * out_scale
        o_ref[0] = o.astype(BF16)

    @pl.when(j < n_ctx_tiles)
    def _():
        run(lc)

    @pl.when(j >= n_ctx_tiles)
    def _():
        run(k_ref.shape[1])


def diff_attention(q, k, v, lam, subln_g, lc, lam_init):
    b, lt, _ = q.shape
    tm = TOKEN_TILE
    nj = lt // tm
    hd = 2 * QK_HEAD_DIM
    kern = functools.partial(_attn_kernel, n_ctx_tiles=lc // tm, lc=lc, out_scale=1.0 - lam_init)
    return pl.pallas_call(
        kern,
        grid=(b, N_HEADS, nj),
        in_specs=[
            pl.BlockSpec(memory_space=pltpu.SMEM),
            pl.BlockSpec((1, tm, hd), lambda i, h, j: (i, j, h)),
            pl.BlockSpec((1, lt, hd), lambda i, h, j: (i, 0, h)),
            pl.BlockSpec((1, lt, V_HEAD_DIM), lambda i, h, j: (i, 0, h)),
            pl.BlockSpec((1, V_HEAD_DIM), lambda i, h, j: (0, 0)),
        ],
        out_specs=pl.BlockSpec((1, tm, V_HEAD_DIM), lambda i, h, j: (i, j, h)),
        out_shape=jax.ShapeDtypeStruct((b, lt, N_HEADS * V_HEAD_DIM), BF16),
        compiler_params=_cparams(("arbitrary", "arbitrary", "arbitrary")),
        name="diff_attention",
    )(lam.reshape(1), q, k, v, subln_g.reshape(1, V_HEAD_DIM))


def _route(logits_t, rb):
    aff = jax.nn.sigmoid(logits_t)
    sel = aff + rb
    s = [sel[e:e + 1] for e in range(N_EXPERTS)]
    a = [aff[e:e + 1] for e in range(N_EXPERTS)]
    npg = EXPERTS_PER_GROUP
    gscore = []
    for g in range(N_EXPERT_GROUPS):
        best = None
        for (i, j) in PAIRS:
            ps = s[g * npg + i] + s[g * npg + j]
            best = ps if best is None else jnp.maximum(best, ps)
        gscore.append(best)
    g_sel = jnp.zeros_like(gscore[0], dtype=jnp.int32)
    g_best = gscore[0]
    for g in range(1, N_EXPERT_GROUPS):
        take = gscore[g] > g_best
        g_sel = jnp.where(take, g, g_sel)
        g_best = jnp.where(take, gscore[g], g_best)

    def pick(rows, i):
        out = rows[i]
        for g in range(1, N_EXPERT_GROUPS):
            out = jnp.where(g_sel == g, rows[g * npg + i], out)
        return out

    xs = [pick(s, i) for i in range(npg)]
    xa = [pick(a, i) for i in range(npg)]
    i1 = jnp.zeros_like(g_sel)
    v1 = xs[0]
    for i in range(1, npg):
        take = xs[i] > v1
        i1 = jnp.where(take, i, i1)
        v1 = jnp.where(take, xs[i], v1)
    i2 = jnp.full_like(g_sel, -1)
    v2 = jnp.full_like(v1, -jnp.inf)
    for i in range(npg):
        take = (i1 != i) & ((i2 < 0) | (xs[i] > v2))
        i2 = jnp.where(take, i, i2)
        v2 = jnp.where(take, xs[i], v2)
    lo = jnp.minimum(i1, i2)
    hi = jnp.maximum(i1, i2)
    pair = jnp.zeros_like(lo)
    a_lo = xa[0]
    a_hi = xa[1]
    for pi_, (i, j) in enumerate(PAIRS):
        hit = (lo == i) & (hi == j)
        pair = jnp.where(hit, pi_, pair)
        a_lo = jnp.where(hit, xa[i], a_lo)
        a_hi = jnp.where(hit, xa[j], a_hi)
    tot = a_lo + a_hi
    cls = g_sel * len(PAIRS) + pair
    return cls, a_lo / tot, a_hi / tot


def _merge_kernel(x_ref, yg_ref, o_ref, gate_ref, mod_ref, g2_ref, wglu_ref, wat_ref, wo_ref,
                  rw_ref, rb_ref, xo_ref, h2_ref, route_ref):
    d = x_ref.shape[-1]
    m = mod_ref[0]
    ab = _dot(yg_ref[0], wglu_ref[...])
    p_s = ab[:, 0:d] * jax.nn.sigmoid(ab[:, d:2 * d])
    p_a = _dot(o_ref[0], wat_ref[...])
    gate = gate_ref[0]
    mix = gate[:, 0:d].astype(F32) * p_s + gate[:, d:2 * d].astype(F32) * p_a
    x = x_ref[0] + m[2:3] * _dot(mix.astype(BF16), wo_ref[...])
    xo_ref[0] = x
    ms = jnp.mean(x * x, axis=-1, keepdims=True)
    y = x * lax.rsqrt(ms + NORM_EPS)
    h2 = (y * g2_ref[...]) * (1.0 + m[4:5]) + m[3:4]
    h2_ref[...] = h2
    logits = jnp.dot(h2, rw_ref[...], precision=HIGHEST, preferred_element_type=F32)
    lt = logits.T[0:N_EXPERTS]
    cls, w_lo, w_hi = _route(lt, rb_ref[...])
    pad = jnp.zeros((8 - 3,) + cls.shape[1:], F32)
    route_ref[...] = jnp.concatenate([cls.astype(F32), w_lo, w_hi, pad], axis=0)


def merge_and_route(xx, yg, o, gate, mods, g2, wglu, wat, wo, rw_pad, rb, n_ctx_tiles):
    b, lt, d = xx.shape
    tm = TOKEN_TILE
    nj = lt // tm
    t = b * lt
    tok = lambda i, j: (i, j, 0)
    const = lambda i, j: (0, 0)

    def mod_idx(i, j):
        return (jnp.where(j < n_ctx_tiles, b, i), 0, 0)

    return pl.pallas_call(
        _merge_kernel,
        grid=(b, nj),
        in_specs=[
            pl.BlockSpec((1, tm, d), tok),
            pl.BlockSpec((1, tm, yg.shape[-1]), tok),
            pl.BlockSpec((1, tm, o.shape[-1]), tok),
            pl.BlockSpec((1, tm, gate.shape[-1]), tok),
            pl.BlockSpec((1, 6, d), mod_idx),
            pl.BlockSpec((1, d), const),
            pl.BlockSpec(wglu.shape, const),
            pl.BlockSpec(wat.shape, const),
            pl.BlockSpec(wo.shape, const),
            pl.BlockSpec(rw_pad.shape, const),
            pl.BlockSpec(rb.shape, const),
        ],
        out_specs=[
            pl.BlockSpec((1, tm, d), tok),
            pl.BlockSpec((tm, d), lambda i, j: (i * nj + j, 0)),
            pl.BlockSpec((8, tm), lambda i, j: (0, i * nj + j)),
        ],
        out_shape=[
            jax.ShapeDtypeStruct((b, lt, d), F32),
            jax.ShapeDtypeStruct((t, d), F32),
            jax.ShapeDtypeStruct((8, t), F32),
        ],
        input_output_aliases={0: 0},
        compiler_params=_cparams(("arbitrary", "arbitrary")),
        name="merge_and_route",
    )(xx, yg, o, gate, mods, g2.reshape(1, d), wglu, wat, wo, rw_pad, rb)


def dispatch_plan(cls, w_pair):
    t = cls.shape[0]
    r = EXPERT_BLOCK
    ncls = N_CLASSES
    cap = -(-(t + ncls * (r - 1)) // r) * r
    nblk = cap // r
    oh = (cls[:, None] == jnp.arange(ncls, dtype=jnp.int32)[None, :]).astype(jnp.int32)
    csum = jnp.cumsum(oh, axis=0)
    counts = csum[-1]
    rank = jnp.sum((csum - oh) * oh, axis=1)
    padded = (counts + r - 1) // r * r
    pend = jnp.cumsum(padded)
    pstart = pend - padded
    pos = jnp.sum(oh * pstart[None, :], axis=1) + rank
    slot_tok = jnp.zeros((cap,), jnp.int32).at[pos].set(jnp.arange(t, dtype=jnp.int32))
    slot_w = jnp.zeros((cap, 2), F32).at[pos].set(w_pair)
    n_valid = pend[-1] // r
    blk = jnp.arange(nblk, dtype=jnp.int32)
    blk_cls = jnp.minimum(jnp.searchsorted(pend, blk * r, side='right'), ncls - 1).astype(jnp.int32)
    last_cls = jnp.max(jnp.where(counts > 0, jnp.arange(ncls, dtype=jnp.int32), 0))
    blk_cls = jnp.where(blk < n_valid, blk_cls, last_cls)
    pair_lo = jnp.array([p[0] for p in PAIRS], jnp.int32)
    pair_hi = jnp.array([p[1] for p in PAIRS], jnp.int32)
    grp = blk_cls // len(PAIRS)
    pr = blk_cls % len(PAIRS)
    e_lo = grp * EXPERTS_PER_GROUP + pair_lo[pr]
    e_hi = grp * EXPERTS_PER_GROUP + pair_hi[pr]
    return pos.astype(jnp.int32), slot_tok, slot_w, e_lo, e_hi, n_valid.astype(jnp.int32).reshape(1)


def _row_copy(tab_ref, row, dst_ref, r, sem):
    return pltpu.make_async_copy(tab_ref.at[pl.ds(row, 1)], dst_ref.at[pl.ds(r, 1)], sem)


def _gather_rows(idx_ref, tab_ref, dst_ref, sem, n):
    def issue(r, carry):
        _row_copy(tab_ref, idx_ref[0, 0, r], dst_ref, r, sem).start()
        return carry
    lax.fori_loop(0, n, issue, 0)
    pltpu.make_async_copy(tab_ref.at[pl.ds(0, n)], dst_ref, sem).wait()


def _gather_kernel(idx_ref, tab_ref, o_ref, sem):
    _gather_rows(idx_ref, tab_ref, o_ref, sem, o_ref.shape[0])


def gather_rows(table, idx):
    n_out = idx.shape[0]
    r = EXPERT_BLOCK
    d = table.shape[-1]
    nblk = n_out // r
    return pl.pallas_call(
        _gather_kernel,
        grid=(nblk,),
        in_specs=[
            pl.BlockSpec((1, 1, r), lambda i: (i, 0, 0), memory_space=pltpu.SMEM),
            pl.BlockSpec(memory_space=pl.ANY),
        ],
        out_specs=pl.BlockSpec((r, d), lambda i: (i, 0)),
        out_shape=jax.ShapeDtypeStruct((n_out, d), table.dtype),
        scratch_shapes=[pltpu.SemaphoreType.DMA(())],
        compiler_params=_cparams(("arbitrary",)),
        name="gather_rows",
    )(idx.reshape(nblk, 1, r), table)


def _combine_kernel(idx_ref, tab_ref, x_ref, mod_ref, o_ref, buf_ref, sem):
    _gather_rows(idx_ref, tab_ref, buf_ref, sem, buf_ref.shape[0])
    o_ref[0] = x_ref[0] + mod_ref[0][5:6] * buf_ref[...]


def combine_residual(xx, ys, pos, mods, n_ctx_tiles):
    b, lt, d = xx.shape
    tm = TOKEN_TILE
    nj = lt // tm
    tok = lambda i, j: (i, j, 0)

    def mod_idx(i, j):
        return (jnp.where(j < n_ctx_tiles, b, i), 0, 0)

    return pl.pallas_call(
        _combine_kernel,
        grid=(b, nj),
        in_specs=[
            pl.BlockSpec((1, 1, tm), lambda i, j: (i * nj + j, 0, 0), memory_space=pltpu.SMEM),
            pl.BlockSpec(memory_space=pl.ANY),
            pl.BlockSpec((1, tm, d), tok),
            pl.BlockSpec((1, 6, d), mod_idx),
        ],
        out_specs=pl.BlockSpec((1, tm, d), tok),
        out_shape=jax.ShapeDtypeStruct((b, lt, d), F32),
        scratch_shapes=[pltpu.VMEM((tm, d), F32), pltpu.SemaphoreType.DMA(())],
        input_output_aliases={2: 0},
        compiler_params=_cparams(("arbitrary", "arbitrary")),
        name="combine_residual",
    )(pos.reshape(b * nj, 1, tm), ys, xx, mods)


def _expert_kernel(elo_ref, ehi_ref, nv_ref, x_ref, wt_ref, w1a, w3a, w2a, w1b, w3b, w2b, o_ref):
    i = pl.program_id(0)

    @pl.when(i < nv_ref[0])
    def _():
        xb = x_ref[...].astype(BF16)

        def ffn(w1, w3, w2):
            h1 = _dot(xb, w1[0])
            h3 = _dot(xb, w3[0])
            hid = (h1 * jax.nn.sigmoid(h1)) * h3
            return _dot(hid.astype(BF16), w2[0])

        wt = wt_ref[...]
        o_ref[...] = wt[:, 0:1] * ffn(w1a, w3a, w2a) + wt[:, 1:2] * ffn(w1b, w3b, w2b)

    @pl.when(i >= nv_ref[0])
    def _():
        o_ref[...] = jnp.zeros_like(o_ref)


def expert_blocks(xs, slot_w, e_lo, e_hi, n_valid, w1, w3, w2):
    cap, d = xs.shape
    r = EXPERT_BLOCK
    nblk = cap // r
    de = w1.shape[-1]
    row = lambda i, elo, ehi, nv: (i, 0)
    wlo = lambda i, elo, ehi, nv: (elo[i], 0, 0)
    whi = lambda i, elo, ehi, nv: (ehi[i], 0, 0)
    grid_spec = pltpu.PrefetchScalarGridSpec(
        num_scalar_prefetch=3,
        grid=(nblk,),
        in_specs=[
            pl.BlockSpec((r, d), row),
            pl.BlockSpec((r, 2), row),
            pl.BlockSpec((1, d, de), wlo),
            pl.BlockSpec((1, d, de), wlo),
            pl.BlockSpec((1, de, d), wlo),
            pl.BlockSpec((1, d, de), whi),
            pl.BlockSpec((1, d, de), whi),
            pl.BlockSpec((1, de, d), whi),
        ],
        out_specs=pl.BlockSpec((r, d), row),
    )
    return pl.pallas_call(
        _expert_kernel,
        grid_spec=grid_spec,
        out_shape=jax.ShapeDtypeStruct((cap, d), F32),
        compiler_params=_cparams(("arbitrary",)),
        name="expert_blocks",
    )(e_lo, e_hi, n_valid, xs, slot_w, w1, w3, w2, w1, w3, w2)


def _final_norm_kernel(x_ref, g_ref, o_ref):
    x = x_ref[0]
    ms = jnp.mean(x * x, axis=-1, keepdims=True)
    o_ref[0] = (x * lax.rsqrt(ms + NORM_EPS)) * g_ref[...]


def final_norm(xx, g, lc):
    b, lt, d = xx.shape
    tm = TOKEN_TILE
    s = lt - lc
    off = lc // tm
    return pl.pallas_call(
        _final_norm_kernel,
        grid=(b, s // tm),
        in_specs=[
            pl.BlockSpec((1, tm, d), lambda i, j: (i, j + off, 0)),
            pl.BlockSpec((1, d), lambda i, j: (0, 0)),
        ],
        out_specs=pl.BlockSpec((1, tm, d), lambda i, j: (i, j, 0)),
        out_shape=jax.ShapeDtypeStruct((b, s, d), F32),
        compiler_params=_cparams(("arbitrary", "arbitrary")),
        name="final_norm",
    )(xx, g.reshape(1, d))


def kernel(x, c, ctx, c_ctx, ada_w, ada_b, norm1_g, norm2_g, final_g, w_in,
           ssm_lam_re, ssm_lam_im, ssm_log_dt, ssm_b_re, ssm_b_im, ssm_c_re, ssm_c_im, ssm_d, w_glu,
           attn_lambda, attn_subln_g, w_attn_out, w_o, router_w, router_b, exp_w1, exp_w3, exp_w2):
    b, s, d = x.shape
    lc = ctx.shape[1]
    depth = w_in.shape[0]
    tm = TOKEN_TILE
    assert lc % tm == 0 and s % tm == 0 and s % GRID_W == 0 and lc % SSM_CHUNK == 0
    n_ctx_tiles = lc // tm
    lt = lc + s

    xx = jnp.concatenate([ctx, x], axis=1)

    n_mod_rows = -(-(b + 1) // 8) * 8
    c_all = jnp.zeros((n_mod_rows, d), F32).at[:b].set(c).at[b].set(c_ctx)
    mods_all = ada_modulation(c_all, ada_w, ada_b).reshape(depth, n_mod_rows, 6, d)

    cos_t, sa_t, sb_t = rope_tables(lc, s)
    rw_pad = jnp.zeros((d, 128), F32).at[:, :N_EXPERTS].set(router_w.astype(F32))
    rb = router_b.astype(F32).reshape(N_EXPERTS, 1)

    for l in range(depth):
        lam_init = 0.8 - 0.6 * math.exp(-0.3 * l)
        mods = mods_all[l]
        u, q, k, v, gate = in_projection(xx, mods, norm1_g[l], w_in[l].astype(BF16),
                                         cos_t, sa_t, sb_t, n_ctx_tiles)

        w1s, w2s, a16 = s5_chunk_weights(ssm_lam_re[l], ssm_lam_im[l], ssm_log_dt[l], ssm_b_re[l], ssm_b_im[l],
                                         ssm_c_re[l], ssm_c_im[l], ssm_d[l])
        yg = s5_mixer(u, w1s, w2s, a16, lc)

        lf = attn_lambda[l].astype(F32)
        lam = jnp.exp(jnp.sum(lf[0] * lf[1])) - jnp.exp(jnp.sum(lf[2] * lf[3])) + lam_init
        o = diff_attention(q, k, v, lam, attn_subln_g[l].astype(F32), lc, lam_init)

        xx, h2, route = merge_and_route(xx, yg, o, gate, mods, norm2_g[l],
                                        w_glu[l].astype(BF16), w_attn_out[l].astype(BF16), w_o[l].astype(BF16),
                                        rw_pad, rb, n_ctx_tiles)

        cls = route[0].astype(jnp.int32)
        pos, slot_tok, slot_w, e_lo, e_hi, n_valid = dispatch_plan(cls, route[1:3].T)
        xs = gather_rows(h2, slot_tok)
        ys = expert_blocks(xs, slot_w, e_lo, e_hi, n_valid,
                           exp_w1[l].astype(BF16), exp_w3[l].astype(BF16), exp_w2[l].astype(BF16))
        xx = combine_residual(xx, ys, pos, mods, n_ctx_tiles)

    return final_norm(xx, final_g, lc)
```

```python
import functools
import math

import jax
import jax.numpy as jnp
from jax import lax
from jax.experimental import pallas as pl
from jax.experimental.pallas import tpu as pltpu

F32 = jnp.float32
BF16 = jnp.bfloat16
HIGHEST = lax.Precision.HIGHEST

GRID_W = 64
SSM_GROUP = 16
SSM_STATE = 64
N_HEADS = 4
QK_HEAD_DIM = 64
V_HEAD_DIM = 128
ROPE_BASE = 10000.0
ROPE_AXIS_FREQS = QK_HEAD_DIM // 4
N_EXPERTS = 16
EXPERTS_PER_GROUP = 4
N_EXPERT_GROUPS = N_EXPERTS // EXPERTS_PER_GROUP
NORM_EPS = 1e-6
SUBLN_EPS = 1e-5

PAIRS = ((0, 1), (0, 2), (0, 3), (1, 2), (1, 3), (2, 3))
N_CLASSES = N_EXPERT_GROUPS * len(PAIRS)

TOKEN_TILE = 256
SSM_CHUNK = 16
EXPERT_BLOCK = 256
VMEM_LIMIT = 56 * 1024 * 1024


def _dot(a, b):
    return jnp.dot(a, b, preferred_element_type=F32)


def _cparams(sem):
    return pltpu.CompilerParams(dimension_semantics=sem, vmem_limit_bytes=VMEM_LIMIT)


def _ada_kernel(c_ref, w_ref, b_ref, o_ref):
    c = c_ref[...]
    s = c * jax.nn.sigmoid(c)
    o_ref[0] = jnp.dot(s, w_ref[0], precision=HIGHEST, preferred_element_type=F32) + b_ref[0]


def ada_modulation(c_all, ada_w, ada_b):
    depth, d, n = ada_w.shape
    r = c_all.shape[0]
    tn = n // 4
    return pl.pallas_call(
        _ada_kernel,
        grid=(depth, n // tn),
        in_specs=[
            pl.BlockSpec((r, d), lambda l, j: (0, 0)),
            pl.BlockSpec((1, d, tn), lambda l, j: (l, 0, j)),
            pl.BlockSpec((1, 1, tn), lambda l, j: (l, 0, j)),
        ],
        out_specs=pl.BlockSpec((1, r, tn), lambda l, j: (l, 0, j)),
        out_shape=jax.ShapeDtypeStruct((depth, r, n), F32),
        compiler_params=_cparams(("arbitrary", "arbitrary")),
        name="ada_modulation",
    )(c_all, ada_w, ada_b.reshape(depth, 1, n))


def _inproj_kernel(x_ref, mod_ref, g_ref, w_ref, wvt_ref, cos_ref, sa_ref, sb_ref,
                   u_ref, q_ref, k_ref, v_ref, gate_ref, *, d_ssm, d_qk, d_v):
    x = x_ref[0]
    m = mod_ref[0]
    ms = jnp.mean(x * x, axis=-1, keepdims=True)
    y = x * lax.rsqrt(ms + NORM_EPS)
    h = (y * g_ref[...]) * (1.0 + m[1:2]) + m[0:1]
    hb = h.astype(BF16)

    cos = cos_ref[...]
    sa = sa_ref[...]
    sb = sb_ref[...]
    half = ROPE_AXIS_FREQS

    def rope(t):
        return t * cos + pltpu.roll(t, d_qk - half, 1) * sa + pltpu.roll(t, half, 1) * sb

    o0 = 0
    u = _dot(hb, w_ref[:, o0:o0 + d_ssm]).astype(BF16)
    c = SSM_CHUNK
    for ci in range(u.shape[0] // c):
        u_ref[ci, 0] = u[ci * c:(ci + 1) * c, :]
    o0 += d_ssm
    q_ref[0] = (rope(_dot(hb, w_ref[:, o0:o0 + d_qk])) * (QK_HEAD_DIM ** -0.5 * math.log2(math.e))).astype(BF16)
    o0 += d_qk
    k_ref[0] = rope(_dot(hb, w_ref[:, o0:o0 + d_qk])).astype(BF16)
    o0 += d_qk
    v_ref[0] = lax.dot_general(wvt_ref[...], hb, (((1,), (1,)), ((), ())),
                               preferred_element_type=F32).astype(BF16)
    o0 += d_v
    n_gate = gate_ref.shape[-1]
    gate_ref[0] = jax.nn.sigmoid(_dot(hb, w_ref[:, o0:o0 + n_gate])).astype(BF16)


def in_projection(xx, mods, g, w_bf, cos_t, sa_t, sb_t, n_ctx_tiles):
    b, lt, d = xx.shape
    nj = lt // TOKEN_TILE
    n_lat = b
    d_ssm = d // 2
    d_qk = N_HEADS * 2 * QK_HEAD_DIM
    d_v = N_HEADS * V_HEAD_DIM
    n_gate = 2 * d
    tm = TOKEN_TILE

    def mod_idx(i, j):
        return (jnp.where(j < n_ctx_tiles, n_lat, i), 0, 0)

    tok = lambda i, j: (i, j, 0)
    kern = functools.partial(_inproj_kernel, d_ssm=d_ssm, d_qk=d_qk, d_v=d_v)
    v0 = d_ssm + 2 * d_qk
    wvt = w_bf[:, v0:v0 + d_v].T
    return pl.pallas_call(
        kern,
        grid=(b, nj),
        in_specs=[
            pl.BlockSpec((1, tm, d), tok),
            pl.BlockSpec((1, 6, d), mod_idx),
            pl.BlockSpec((1, d), lambda i, j: (0, 0)),
            pl.BlockSpec(w_bf.shape, lambda i, j: (0, 0)),
            pl.BlockSpec(wvt.shape, lambda i, j: (0, 0)),
            pl.BlockSpec((tm, d_qk), lambda i, j: (j, 0)),
            pl.BlockSpec((tm, d_qk), lambda i, j: (j, 0)),
            pl.BlockSpec((tm, d_qk), lambda i, j: (j, 0)),
        ],
        out_specs=[
            pl.BlockSpec((tm // SSM_CHUNK, 1, SSM_CHUNK, d_ssm), lambda i, j: (j, i, 0, 0)),
            pl.BlockSpec((1, tm, d_qk), tok),
            pl.BlockSpec((1, tm, d_qk), tok),
            pl.BlockSpec((1, d_v, tm), lambda i, j: (i, 0, j)),
            pl.BlockSpec((1, tm, n_gate), tok),
        ],
        out_shape=[
            jax.ShapeDtypeStruct((lt // SSM_CHUNK, b, SSM_CHUNK, d_ssm), BF16),
            jax.ShapeDtypeStruct((b, lt, d_qk), BF16),
            jax.ShapeDtypeStruct((b, lt, d_qk), BF16),
            jax.ShapeDtypeStruct((b, d_v, lt), BF16),
            jax.ShapeDtypeStruct((b, lt, n_gate), BF16),
        ],
        compiler_params=_cparams(("arbitrary", "arbitrary")),
        name="in_projection",
    )(xx, mods, g.reshape(1, d), w_bf, wvt, cos_t, sa_t, sb_t)


def rope_tables(lc, s):
    rows = s // GRID_W
    row = jnp.repeat(jnp.arange(rows), GRID_W).astype(F32)
    col = jnp.tile(jnp.arange(GRID_W), rows).astype(F32)
    inv = ROPE_BASE ** (-jnp.arange(ROPE_AXIS_FREQS, dtype=F32) / ROPE_AXIS_FREQS)
    ang_r = row[:, None] * inv
    ang_c = col[:, None] * inv
    ang = jnp.concatenate([ang_r, ang_r, ang_c, ang_c], axis=-1)
    cos = jnp.concatenate([jnp.ones((lc, QK_HEAD_DIM), F32), jnp.cos(ang)], axis=0)
    sin = jnp.concatenate([jnp.zeros((lc, QK_HEAD_DIM), F32), jnp.sin(ang)], axis=0)
    first_half = (jnp.arange(QK_HEAD_DIM) % (2 * ROPE_AXIS_FREQS)) < ROPE_AXIS_FREQS
    sa = jnp.where(first_half, -sin, 0.0)
    sb = jnp.where(first_half, 0.0, sin)
    reps = N_HEADS * 2
    return jnp.tile(cos, (1, reps)), jnp.tile(sa, (1, reps)), jnp.tile(sb, (1, reps))


def _cmul(ar, ai, br, bi):
    return ar * br - ai * bi, ar * bi + ai * br


def s5_chunk_weights(lam_re, lam_im, log_dt, b_re, b_im, c_re, c_im, d_skip):
    c = SSM_CHUNK
    lr = lam_re.astype(F32)
    li = lam_im.astype(F32)
    dt = jnp.exp(log_dt.astype(F32))[..., None]
    decay = jnp.exp(lr * dt)
    ab_re = decay * jnp.cos(li * dt)
    ab_im = decay * jnp.sin(li * dt)
    den = lr * lr + li * li
    nr = ab_re - 1.0
    ni = ab_im
    cr = ((nr * lr + ni * li) / den)[..., None]
    ci = ((ni * lr - nr * li) / den)[..., None]
    br = b_re.astype(F32)
    bi = b_im.astype(F32)
    bb_re = cr * br - ci * bi
    bb_im = cr * bi + ci * br
    ccr = c_re.astype(F32)
    cci = c_im.astype(F32)

    pr = [jnp.ones_like(ab_re)]
    pi = [jnp.zeros_like(ab_im)]
    for _ in range(c):
        r, i = _cmul(pr[-1], pi[-1], ab_re, ab_im)
        pr.append(r)
        pi.append(i)
    pw_re = jnp.stack(pr)
    pw_im = jnp.stack(pi)

    pb_re, pb_im = _cmul(pw_re[..., None], pw_im[..., None], bb_re[None], bb_im[None])
    cp_re, cp_im = _cmul(ccr[None], cci[None], pw_re[:, :, :, None, :], pw_im[:, :, :, None, :])
    kk = (jnp.einsum('dgop,tdgpi->tdgoi', ccr, pb_re[:c], precision=HIGHEST)
          - jnp.einsum('dgop,tdgpi->tdgoi', cci, pb_im[:c], precision=HIGHEST))

    g = lr.shape[1]
    h = SSM_GROUP
    p = SSM_STATE
    ii = jnp.arange(c)[:, None]
    jj = jnp.arange(c)[None, :]
    lag_f = jnp.clip(jj - ii, 0, c - 1)
    lag_b = jnp.clip(ii - jj, 0, c - 1)
    kf = kk[:, 0]
    kb = kk[:, 1]
    tf = jnp.where((jj >= ii)[:, :, None, None, None], kf[lag_f], 0.0)
    tb = jnp.where((ii >= jj)[:, :, None, None, None], kb[lag_b], 0.0)
    tt = jnp.transpose(tf + tb, (2, 0, 4, 1, 3))
    dd = d_skip.astype(F32).reshape(g, h)
    eye_t = jnp.eye(c, dtype=F32)
    eye_h = jnp.eye(h, dtype=F32)
    tt = tt + (eye_t[None, :, None, :, None] * eye_h[None, None, :, None, :]
               * dd[:, None, :, None, None])
    tt = tt.reshape(g, c * h, c * h)

    def s_cols(pbx, d, rev):
        m = pbx[:c, d]
        if rev:
            m = m[::-1]
        return jnp.transpose(m, (1, 0, 3, 2)).reshape(g, c * h, p)

    w1 = jnp.concatenate([
        tt,
        s_cols(pb_re, 0, True), s_cols(pb_re, 1, False),
        s_cols(pb_im, 0, True), s_cols(pb_im, 1, False)], axis=-1)

    def o_rows(cpx, d, sign, fwd):
        m = cpx[1:c + 1, d]
        if not fwd:
            m = m[::-1]
        return sign * jnp.transpose(m, (1, 3, 0, 2)).reshape(g, p, c * h)

    w2 = jnp.concatenate([
        o_rows(cp_re, 0, 1.0, True), o_rows(cp_re, 1, 1.0, False),
        o_rows(cp_im, 0, -1.0, True), o_rows(cp_im, 1, -1.0, False)], axis=1)

    a16 = jnp.zeros((g, 8, 2 * p), F32)
    a16 = a16.at[:, 0, :].set(jnp.concatenate([pw_re[c, 0], pw_re[c, 1]], axis=-1))
    a16 = a16.at[:, 1, :].set(jnp.concatenate([pw_im[c, 0], pw_im[c, 1]], axis=-1))
    return w1.astype(BF16), w2.astype(BF16), a16


def _s5_kernel(x_ref, w1_ref, w2_ref, a_ref, y_ref, z_ref, hin_ref, *, nb, nc_ctx, nc_lat):
    ch = x_ref.shape[-1]
    p = SSM_STATE
    z_ref[...] = _dot(x_ref[0], w1_ref[0])
    a_re = a_ref[0, 0:1, :]
    a_im = a_ref[0, 1:2, :]
    fwd_lane = lax.broadcasted_iota(jnp.int32, (nb, 2 * p), 1) < p

    def phase(base, n, carry):
        def step(i, hc):
            h_re, h_im = hc
            rf = pl.multiple_of((base + i) * nb, nb)
            rb = pl.multiple_of((base + n - 1 - i) * nb, nb)
            hin_ref[pl.ds(rf, nb), 0:p] = h_re[:, 0:p]
            hin_ref[pl.ds(rb, nb), p:2 * p] = h_re[:, p:2 * p]
            hin_ref[pl.ds(rf, nb), 2 * p:3 * p] = h_im[:, 0:p]
            hin_ref[pl.ds(rb, nb), 3 * p:4 * p] = h_im[:, p:2 * p]
            s_re = jnp.where(fwd_lane, z_ref[pl.ds(rf, nb), ch:ch + 2 * p],
                             z_ref[pl.ds(rb, nb), ch:ch + 2 * p])
            s_im = jnp.where(fwd_lane, z_ref[pl.ds(rf, nb), ch + 2 * p:ch + 4 * p],
                             z_ref[pl.ds(rb, nb), ch + 2 * p:ch + 4 * p])
            n_re = a_re * h_re - a_im * h_im + s_re
            n_im = a_re * h_im + a_im * h_re + s_im
            return n_re, n_im
        return lax.fori_loop(0, n, step, carry)

    zero = jnp.zeros((nb, 2 * p), F32)
    hc = phase(0, nc_ctx, (zero, zero))
    phase(nc_ctx, nc_lat, hc)
    y = z_ref[:, 0:ch] + _dot(hin_ref[...].astype(BF16), w2_ref[0])
    y_ref[0] = jax.nn.gelu(y).astype(BF16)


def _regroup_kernel(x_ref, o_ref, *, n_groups):
    h = SSM_GROUP
    for t in range(SSM_CHUNK):
        for g in range(n_groups):
            src = (t * n_groups + g) * h
            o_ref[g, :, t * h:(t + 1) * h] = x_ref[:, src:src + h]


def _ungroup_kernel(y_ref, o_ref, *, n_groups):
    h = SSM_GROUP
    for t in range(SSM_CHUNK):
        for g in range(n_groups):
            dst = (t * n_groups + g) * h
            o_ref[:, dst:dst + h] = y_ref[g, :, t * h:(t + 1) * h]


REGROUP_ROWS = 128


def _regroup(x2, n_groups, inverse):
    c = SSM_CHUNK
    h = SSM_GROUP
    if inverse:
        g, m, _ = x2.shape
    else:
        m = x2.shape[0]
        g = n_groups
    rb = min(REGROUP_ROWS, m)
    assert m % rb == 0
    wide = pl.BlockSpec((rb, c * g * h), lambda i: (i, 0))
    grouped = pl.BlockSpec((g, rb, c * h), lambda i: (0, i, 0))
    kern = functools.partial(_ungroup_kernel if inverse else _regroup_kernel, n_groups=g)
    return pl.pallas_call(
        kern,
        grid=(m // rb,),
        in_specs=[grouped if inverse else wide],
        out_specs=wide if inverse else grouped,
        out_shape=jax.ShapeDtypeStruct((m, c * g * h) if inverse else (g, m, c * h), x2.dtype),
        compiler_params=_cparams(("arbitrary",)),
        name="s5_ungroup" if inverse else "s5_regroup",
    )(x2)


def s5_mixer(u4, w1, w2, a16, lc):
    nc, b, c, dssm = u4.shape
    h = SSM_GROUP
    g = dssm // h
    lt = nc * c
    m = nc * b
    ug = _regroup(u4.reshape(m, c * dssm), g, False)
    kern = functools.partial(_s5_kernel, nb=b, nc_ctx=lc // c, nc_lat=(lt - lc) // c)
    yg = pl.pallas_call(
        kern,
        grid=(g,),
        in_specs=[
            pl.BlockSpec((1, m, c * h), lambda i: (i, 0, 0)),
            pl.BlockSpec((1,) + w1.shape[1:], lambda i: (i, 0, 0)),
            pl.BlockSpec((1,) + w2.shape[1:], lambda i: (i, 0, 0)),
            pl.BlockSpec((1, 8, 2 * SSM_STATE), lambda i: (i, 0, 0)),
        ],
        out_specs=pl.BlockSpec((1, m, c * h), lambda i: (i, 0, 0)),
        out_shape=jax.ShapeDtypeStruct((g, m, c * h), BF16),
        scratch_shapes=[pltpu.VMEM((m, w1.shape[-1]), F32), pltpu.VMEM((m, 4 * SSM_STATE), F32)],
        compiler_params=_cparams(("arbitrary",)),
        name="s5_mixer",
    )(ug, w1, w2, a16)
    return _regroup(yg, g, True).reshape(nc, b, c, dssm)


KEY_CHUNK = 128


def _attn_kernel(lam_ref, q_ref, k_ref, vt_ref, g_ref, o_ref, *, n_ctx_tiles, lc, out_scale):
    j = pl.program_id(2)
    lam = lam_ref[0]
    q = q_ref[0]
    lo = lax.broadcasted_iota(jnp.int32, q.shape, 1) < QK_HEAD_DIM
    zq = jnp.zeros_like(q)
    qs = (jnp.where(lo, q, zq), jnp.where(lo, zq, q))
    nt = (((1,), (1,)), ((), ()))
    kc = KEY_CHUNK

    def scores(c, qm):
        return lax.dot_general(k_ref[0, c * kc:(c + 1) * kc, :], qm, nt, preferred_element_type=F32)

    dv = vt_ref.shape[1]
    ones = jnp.ones((16, kc), BF16)

    def run(nk):
        n_chunks = nk // kc
        acc = []
        den = []
        for qm in qs:
            m = None
            for c in range(n_chunks):
                cm = jnp.max(scores(c, qm), axis=0, keepdims=True)
                m = cm if m is None else jnp.maximum(m, cm)
            a = None
            for c in range(n_chunks):
                e = jnp.exp2(scores(c, qm) - m)
                va = jnp.concatenate([vt_ref[0, :, c * kc:(c + 1) * kc], ones], axis=0)
                ca = _dot(va, e.astype(BF16))
                a = ca if a is None else a + ca
            acc.append(a[0:dv])
            den.append(a[dv:dv + 1])
        ot = acc[0] * (1.0 / den[0]) - acc[1] * (lam / den[1])
        ms = jnp.mean(ot * ot, axis=0, keepdims=True)
        ot = (ot * lax.rsqrt(ms + SUBLN_EPS)) * g_ref[...] * out_scale
        o_ref[0] = ot.T.astype(BF16)

    @pl.when(j < n_ctx_tiles)
    def _():
        run(lc)

    @pl.when(j >= n_ctx_tiles)
    def _():
        run(k_ref.shape[1])


def diff_attention(q, k, vt, lam, subln_g, lc, lam_init):
    b, lt, _ = q.shape
    tm = TOKEN_TILE
    nj = lt // tm
    hd = 2 * QK_HEAD_DIM
    assert lc % KEY_CHUNK == 0 and lt % KEY_CHUNK == 0
    kern = functools.partial(_attn_kernel, n_ctx_tiles=lc // tm, lc=lc, out_scale=1.0 - lam_init)
    return pl.pallas_call(
        kern,
        grid=(b, N_HEADS, nj),
        in_specs=[
            pl.BlockSpec(memory_space=pltpu.SMEM),
            pl.BlockSpec((1, tm, hd), lambda i, h, j: (i, j, h)),
            pl.BlockSpec((1, lt, hd), lambda i, h, j: (i, 0, h)),
            pl.BlockSpec((1, V_HEAD_DIM, lt), lambda i, h, j: (i, h, 0)),
            pl.BlockSpec((V_HEAD_DIM, 1), lambda i, h, j: (0, 0)),
        ],
        out_specs=pl.BlockSpec((1, tm, V_HEAD_DIM), lambda i, h, j: (i, j, h)),
        out_shape=jax.ShapeDtypeStruct((b, lt, N_HEADS * V_HEAD_DIM), BF16),
        compiler_params=_cparams(("arbitrary", "arbitrary", "arbitrary")),
        name="diff_attention",
    )(lam.reshape(1), q, k, vt, subln_g.reshape(V_HEAD_DIM, 1))


def _route(logits_t, rb):
    aff = jax.nn.sigmoid(logits_t)
    sel = aff + rb
    s = [sel[e:e + 1] for e in range(N_EXPERTS)]
    a = [aff[e:e + 1] for e in range(N_EXPERTS)]
    npg = EXPERTS_PER_GROUP
    gscore = []
    for g in range(N_EXPERT_GROUPS):
        best = None
        for (i, j) in PAIRS:
            ps = s[g * npg + i] + s[g * npg + j]
            best = ps if best is None else jnp.maximum(best, ps)
        gscore.append(best)
    g_sel = jnp.zeros_like(gscore[0], dtype=jnp.int32)
    g_best = gscore[0]
    for g in range(1, N_EXPERT_GROUPS):
        take = gscore[g] > g_best
        g_sel = jnp.where(take, g, g_sel)
        g_best = jnp.where(take, gscore[g], g_best)

    def pick(rows, i):
        out = rows[i]
        for g in range(1, N_EXPERT_GROUPS):
            out = jnp.where(g_sel == g, rows[g * npg + i], out)
        return out

    xs = [pick(s, i) for i in range(npg)]
    xa = [pick(a, i) for i in range(npg)]
    i1 = jnp.zeros_like(g_sel)
    v1 = xs[0]
    for i in range(1, npg):
        take = xs[i] > v1
        i1 = jnp.where(take, i, i1)
        v1 = jnp.where(take, xs[i], v1)
    i2 = jnp.full_like(g_sel, -1)
    v2 = jnp.full_like(v1, -jnp.inf)
    for i in range(npg):
        take = (i1 != i) & ((i2 < 0) | (xs[i] > v2))
        i2 = jnp.where(take, i, i2)
        v2 = jnp.where(take, xs[i], v2)
    lo = jnp.minimum(i1, i2)
    hi = jnp.maximum(i1, i2)
    pair = jnp.zeros_like(lo)
    a_lo = xa[0]
    a_hi = xa[1]
    for pi_, (i, j) in enumerate(PAIRS):
        hit = (lo == i) & (hi == j)
        pair = jnp.where(hit, pi_, pair)
        a_lo = jnp.where(hit, xa[i], a_lo)
        a_hi = jnp.where(hit, xa[j], a_hi)
    tot = a_lo + a_hi
    cls = g_sel * len(PAIRS) + pair
    return cls, a_lo / tot, a_hi / tot


def _merge_kernel(x_ref, yg_ref, o_ref, gate_ref, mod_ref, g2_ref, wglu_ref, wat_ref, wo_ref,
                  rw_ref, rb_ref, xo_ref, h2_ref, route_ref):
    d = x_ref.shape[-1]
    tm = x_ref.shape[1]
    m = mod_ref[0]
    yg = jnp.concatenate([yg_ref[ci, 0] for ci in range(yg_ref.shape[0])], axis=0)
    ab = _dot(yg, wglu_ref[...])
    p_s = ab[:, 0:d] * jax.nn.sigmoid(ab[:, d:2 * d])
    p_a = _dot(o_ref[0], wat_ref[...])
    gate = gate_ref[0]
    mix = gate[:, 0:d].astype(F32) * p_s + gate[:, d:2 * d].astype(F32) * p_a
    x = x_ref[0] + m[2:3] * _dot(mix.astype(BF16), wo_ref[...])
    xo_ref[0] = x
    ms = jnp.mean(x * x, axis=-1, keepdims=True)
    y = x * lax.rsqrt(ms + NORM_EPS)
    h2 = (y * g2_ref[...]) * (1.0 + m[4:5]) + m[3:4]
    lanes = 128
    for kk in range(d // lanes):
        h2_ref[pl.ds(kk, tm, stride=d // lanes), :] = h2[:, kk * lanes:(kk + 1) * lanes]
    hi = h2.astype(BF16)
    lo = (h2 - hi.astype(F32)).astype(BF16)
    pp = _dot(jnp.concatenate([hi, lo], axis=0), rw_ref[...])
    logits = (pp[0:tm, 0:lanes] + pp[0:tm, lanes:2 * lanes]) + (pp[tm:2 * tm, 0:lanes] + pp[tm:2 * tm, lanes:2 * lanes])
    lt = logits.T[0:N_EXPERTS]
    cls, w_lo, w_hi = _route(lt, rb_ref[...])
    pad = jnp.zeros((8 - 3,) + cls.shape[1:], F32)
    route_ref[...] = jnp.concatenate([cls.astype(F32), w_lo, w_hi, pad], axis=0)


def merge_and_route(xx, yg, o, gate, mods, g2, wglu, wat, wo, rw_pad, rb, n_ctx_tiles):
    b, lt, d = xx.shape
    tm = TOKEN_TILE
    nj = lt // tm
    t = b * lt
    tok = lambda i, j: (i, j, 0)
    const = lambda i, j: (0, 0)

    def mod_idx(i, j):
        return (jnp.where(j < n_ctx_tiles, b, i), 0, 0)

    return pl.pallas_call(
        _merge_kernel,
        grid=(b, nj),
        in_specs=[
            pl.BlockSpec((1, tm, d), tok),
            pl.BlockSpec((tm // SSM_CHUNK, 1, SSM_CHUNK, yg.shape[-1]), lambda i, j: (j, i, 0, 0)),
            pl.BlockSpec((1, tm, o.shape[-1]), tok),
            pl.BlockSpec((1, tm, gate.shape[-1]), tok),
            pl.BlockSpec((1, 6, d), mod_idx),
            pl.BlockSpec((1, d), const),
            pl.BlockSpec(wglu.shape, const),
            pl.BlockSpec(wat.shape, const),
            pl.BlockSpec(wo.shape, const),
            pl.BlockSpec(rw_pad.shape, const),
            pl.BlockSpec(rb.shape, const),
        ],
        out_specs=[
            pl.BlockSpec((1, tm, d), tok),
            pl.BlockSpec((tm * (d // 128), 128), lambda i, j: (i * nj + j, 0)),
            pl.BlockSpec((8, tm), lambda i, j: (0, i * nj + j)),
        ],
        out_shape=[
            jax.ShapeDtypeStruct((b, lt, d), F32),
            jax.ShapeDtypeStruct((t * (d // 128), 128), F32),
            jax.ShapeDtypeStruct((8, t), F32),
        ],
        input_output_aliases={0: 0},
        compiler_params=_cparams(("arbitrary", "arbitrary")),
        name="merge_and_route",
    )(xx, yg, o, gate, mods, g2.reshape(1, d), wglu, wat, wo, rw_pad, rb)


def dispatch_plan(cls, w_pair):
    t = cls.shape[0]
    r = EXPERT_BLOCK
    ncls = N_CLASSES
    cap = -(-(t + ncls * (r - 1)) // r) * r
    nblk = cap // r
    oh = (cls[:, None] == jnp.arange(ncls, dtype=jnp.int32)[None, :]).astype(jnp.int32)
    csum = jnp.cumsum(oh, axis=0)
    counts = csum[-1]
    rank = jnp.sum((csum - oh) * oh, axis=1)
    padded = (counts + r - 1) // r * r
    pend = jnp.cumsum(padded)
    pstart = pend - padded
    pos = jnp.sum(oh * pstart[None, :], axis=1) + rank
    slot_tok = jnp.zeros((cap,), jnp.int32).at[pos].set(jnp.arange(t, dtype=jnp.int32))
    slot_w = jnp.zeros((cap, 2), F32).at[pos].set(w_pair)
    n_valid = pend[-1] // r
    blk = jnp.arange(nblk, dtype=jnp.int32)
    blk_cls = jnp.minimum(jnp.sum((pend[None, :] <= (blk * r)[:, None]).astype(jnp.int32), axis=1), ncls - 1)
    last_cls = jnp.max(jnp.where(counts > 0, jnp.arange(ncls, dtype=jnp.int32), 0))
    blk_cls = jnp.where(blk < n_valid, blk_cls, last_cls)
    pair_lo = jnp.array([p[0] for p in PAIRS], jnp.int32)
    pair_hi = jnp.array([p[1] for p in PAIRS], jnp.int32)
    grp = blk_cls // len(PAIRS)
    pr = blk_cls % len(PAIRS)
    e_lo = grp * EXPERTS_PER_GROUP + pair_lo[pr]
    e_hi = grp * EXPERTS_PER_GROUP + pair_hi[pr]
    return pos.astype(jnp.int32), slot_tok, slot_w, e_lo, e_hi, n_valid.astype(jnp.int32).reshape(1)


TOKEN_SUBLANES = 8
LANES = 128
GATHER_UNROLL = 8


def _token_copy(tab_ref, tok, dst_ref, r, sem):
    ts = TOKEN_SUBLANES
    src = pl.multiple_of(tok * ts, ts)
    dst = pl.multiple_of(r * ts, ts)
    return pltpu.make_async_copy(tab_ref.at[pl.ds(src, ts)], dst_ref.at[pl.ds(dst, ts)], sem)


def _gather_tokens(idx_ref, tab_ref, dst_ref, sem, n):
    def issue(r, carry):
        _token_copy(tab_ref, idx_ref[0, 0, r], dst_ref, r, sem).start()
        return carry
    lax.fori_loop(0, n, issue, 0, unroll=GATHER_UNROLL)
    pltpu.make_async_copy(tab_ref.at[pl.ds(0, n * TOKEN_SUBLANES)], dst_ref, sem).wait()


def _tiles_to_rows(ref, n):
    ts = TOKEN_SUBLANES
    return jnp.concatenate([ref[pl.ds(kk, n, stride=ts), :] for kk in range(ts)], axis=1)


def _gather_kernel(idx_ref, tab_ref, o_ref, sem):
    _gather_tokens(idx_ref, tab_ref, o_ref, sem, o_ref.shape[0] // TOKEN_SUBLANES)


def gather_tokens(table, idx):
    n_out = idx.shape[0]
    r = EXPERT_BLOCK
    ts = TOKEN_SUBLANES
    nblk = n_out // r
    return pl.pallas_call(
        _gather_kernel,
        grid=(nblk,),
        in_specs=[
            pl.BlockSpec((1, 1, r), lambda i: (i, 0, 0), memory_space=pltpu.SMEM),
            pl.BlockSpec(memory_space=pl.ANY),
        ],
        out_specs=pl.BlockSpec((r * ts, LANES), lambda i: (i, 0)),
        out_shape=jax.ShapeDtypeStruct((n_out * ts, LANES), table.dtype),
        scratch_shapes=[pltpu.SemaphoreType.DMA(())],
        compiler_params=_cparams(("arbitrary",)),
        name="gather_tokens",
    )(idx.reshape(nblk, 1, r), table)


def _combine_kernel(idx_ref, tab_ref, x_ref, mod_ref, o_ref, buf_ref, sem):
    n = x_ref.shape[1]
    _gather_tokens(idx_ref, tab_ref, buf_ref, sem, n)
    g2 = mod_ref[0][5:6]
    for kk in range(TOKEN_SUBLANES):
        cols = slice(kk * LANES, (kk + 1) * LANES)
        o_ref[0, :, cols] = x_ref[0, :, cols] + g2[:, cols] * buf_ref[pl.ds(kk, n, stride=TOKEN_SUBLANES), :]


def combine_residual(xx, ys, pos, mods, n_ctx_tiles):
    b, lt, d = xx.shape
    tm = TOKEN_TILE
    nj = lt // tm
    tok = lambda i, j: (i, j, 0)

    def mod_idx(i, j):
        return (jnp.where(j < n_ctx_tiles, b, i), 0, 0)

    return pl.pallas_call(
        _combine_kernel,
        grid=(b, nj),
        in_specs=[
            pl.BlockSpec((1, 1, tm), lambda i, j: (i * nj + j, 0, 0), memory_space=pltpu.SMEM),
            pl.BlockSpec(memory_space=pl.ANY),
            pl.BlockSpec((1, tm, d), tok),
            pl.BlockSpec((1, 6, d), mod_idx),
        ],
        out_specs=pl.BlockSpec((1, tm, d), tok),
        out_shape=jax.ShapeDtypeStruct((b, lt, d), F32),
        scratch_shapes=[pltpu.VMEM((tm * TOKEN_SUBLANES, LANES), F32), pltpu.SemaphoreType.DMA(())],
        input_output_aliases={2: 0},
        compiler_params=_cparams(("arbitrary", "arbitrary")),
        name="combine_residual",
    )(pos.reshape(b * nj, 1, tm), ys, xx, mods)


def _expert_kernel(elo_ref, ehi_ref, nv_ref, x_ref, wt_ref, w1a, w3a, w2a, w1b, w3b, w2b, o_ref):
    i = pl.program_id(0)

    n = wt_ref.shape[0]

    @pl.when(i < nv_ref[0])
    def _():
        xb = _tiles_to_rows(x_ref, n).astype(BF16)

        def ffn(w1, w3, w2):
            h1 = _dot(xb, w1[0])
            h3 = _dot(xb, w3[0])
            hid = (h1 * jax.nn.sigmoid(h1)) * h3
            return _dot(hid.astype(BF16), w2[0])

        wt = wt_ref[...]
        y = wt[:, 0:1] * ffn(w1a, w3a, w2a) + wt[:, 1:2] * ffn(w1b, w3b, w2b)
        for kk in range(TOKEN_SUBLANES):
            o_ref[pl.ds(kk, n, stride=TOKEN_SUBLANES), :] = y[:, kk * LANES:(kk + 1) * LANES]

    @pl.when(i >= nv_ref[0])
    def _():
        o_ref[...] = jnp.zeros_like(o_ref)


def expert_blocks(xs, slot_w, e_lo, e_hi, n_valid, w1, w3, w2):
    ts = TOKEN_SUBLANES
    cap = xs.shape[0] // ts
    d = ts * LANES
    r = EXPERT_BLOCK
    nblk = cap // r
    de = w1.shape[-1]
    row = lambda i, elo, ehi, nv: (i, 0)
    wlo = lambda i, elo, ehi, nv: (elo[i], 0, 0)
    whi = lambda i, elo, ehi, nv: (ehi[i], 0, 0)
    grid_spec = pltpu.PrefetchScalarGridSpec(
        num_scalar_prefetch=3,
        grid=(nblk,),
        in_specs=[
            pl.BlockSpec((r * ts, LANES), row),
            pl.BlockSpec((r, 2), row),
            pl.BlockSpec((1, d, de), wlo),
            pl.BlockSpec((1, d, de), wlo),
            pl.BlockSpec((1, de, d), wlo),
            pl.BlockSpec((1, d, de), whi),
            pl.BlockSpec((1, d, de), whi),
            pl.BlockSpec((1, de, d), whi),
        ],
        out_specs=pl.BlockSpec((r * ts, LANES), row),
    )
    return pl.pallas_call(
        _expert_kernel,
        grid_spec=grid_spec,
        out_shape=jax.ShapeDtypeStruct((cap * ts, LANES), F32),
        compiler_params=_cparams(("arbitrary",)),
        name="expert_blocks",
    )(e_lo, e_hi, n_valid, xs, slot_w, w1, w3, w2, w1, w3, w2)


def _final_norm_kernel(x_ref, g_ref, o_ref):
    x = x_ref[0]
    ms = jnp.mean(x * x, axis=-1, keepdims=True)
    o_ref[0] = (x * lax.rsqrt(ms + NORM_EPS)) * g_ref[...]


def final_norm(xx, g, lc):
    b, lt, d = xx.shape
    tm = TOKEN_TILE
    s = lt - lc
    off = lc // tm
    return pl.pallas_call(
        _final_norm_kernel,
        grid=(b, s // tm),
        in_specs=[
            pl.BlockSpec((1, tm, d), lambda i, j: (i, j + off, 0)),
            pl.BlockSpec((1, d), lambda i, j: (0, 0)),
        ],
        out_specs=pl.BlockSpec((1, tm, d), lambda i, j: (i, j, 0)),
        out_shape=jax.ShapeDtypeStruct((b, s, d), F32),
        compiler_params=_cparams(("arbitrary", "arbitrary")),
        name="final_norm",
    )(xx, g.reshape(1, d))


def kernel(x, c, ctx, c_ctx, ada_w, ada_b, norm1_g, norm2_g, final_g, w_in,
           ssm_lam_re, ssm_lam_im, ssm_log_dt, ssm_b_re, ssm_b_im, ssm_c_re, ssm_c_im, ssm_d, w_glu,
           attn_lambda, attn_subln_g, w_attn_out, w_o, router_w, router_b, exp_w1, exp_w3, exp_w2):
    b, s, d = x.shape
    lc = ctx.shape[1]
    depth = w_in.shape[0]
    tm = TOKEN_TILE
    assert lc % tm == 0 and s % tm == 0 and s % GRID_W == 0 and lc % SSM_CHUNK == 0
    n_ctx_tiles = lc // tm
    lt = lc + s

    xx = jnp.concatenate([ctx, x], axis=1)

    n_mod_rows = -(-(b + 1) // 8) * 8
    c_all = jnp.zeros((n_mod_rows, d), F32).at[:b].set(c).at[b].set(c_ctx)
    mods_all = ada_modulation(c_all, ada_w, ada_b).reshape(depth, n_mod_rows, 6, d)

    cos_t, sa_t, sb_t = rope_tables(lc, s)
    assert d == TOKEN_SUBLANES * LANES
    rw_f = jnp.zeros((d, LANES), F32).at[:, :N_EXPERTS].set(router_w.astype(F32))
    rw_hi = rw_f.astype(BF16)
    rw_lo = (rw_f - rw_hi.astype(F32)).astype(BF16)
    rw_pad = jnp.concatenate([rw_hi, rw_lo], axis=1)
    rb = router_b.astype(F32).reshape(N_EXPERTS, 1)

    for l in range(depth):
        lam_init = 0.8 - 0.6 * math.exp(-0.3 * l)
        mods = mods_all[l]
        u, q, k, v, gate = in_projection(xx, mods, norm1_g[l], w_in[l].astype(BF16),
                                         cos_t, sa_t, sb_t, n_ctx_tiles)

        w1s, w2s, a16 = s5_chunk_weights(ssm_lam_re[l], ssm_lam_im[l], ssm_log_dt[l], ssm_b_re[l], ssm_b_im[l],
                                         ssm_c_re[l], ssm_c_im[l], ssm_d[l])
        yg = s5_mixer(u, w1s, w2s, a16, lc)

        lf = attn_lambda[l].astype(F32)
        lam = jnp.exp(jnp.sum(lf[0] * lf[1])) - jnp.exp(jnp.sum(lf[2] * lf[3])) + lam_init
        o = diff_attention(q, k, v, lam, attn_subln_g[l].astype(F32), lc, lam_init)

        xx, h2, route = merge_and_route(xx, yg, o, gate, mods, norm2_g[l],
                                        w_glu[l].astype(BF16), w_attn_out[l].astype(BF16), w_o[l].astype(BF16),
                                        rw_pad, rb, n_ctx_tiles)

        cls = route[0].astype(jnp.int32)
        pos, slot_tok, slot_w, e_lo, e_hi, n_valid = dispatch_plan(cls, route[1:3].T)
        xs = gather_tokens(h2, slot_tok)
        ys = expert_blocks(xs, slot_w, e_lo, e_hi, n_valid,
                           exp_w1[l].astype(BF16), exp_w3[l].astype(BF16), exp_w2[l].astype(BF16))
        xx = combine_residual(xx, ys, pos, mods, n_ctx_tiles)

    return final_norm(xx, final_g, lc)
```

```python
import functools
import math

import jax
import jax.numpy as jnp
from jax import lax
from jax.experimental import pallas as pl
from jax.experimental.pallas import tpu as pltpu

F32 = jnp.float32
BF16 = jnp.bfloat16
HIGHEST = lax.Precision.HIGHEST

GRID_W = 64
SSM_GROUP = 16
SSM_STATE = 64
N_HEADS = 4
QK_HEAD_DIM = 64
V_HEAD_DIM = 128
ROPE_BASE = 10000.0
ROPE_AXIS_FREQS = QK_HEAD_DIM // 4
N_EXPERTS = 16
EXPERTS_PER_GROUP = 4
N_EXPERT_GROUPS = N_EXPERTS // EXPERTS_PER_GROUP
NORM_EPS = 1e-6
SUBLN_EPS = 1e-5

PAIRS = ((0, 1), (0, 2), (0, 3), (1, 2), (1, 3), (2, 3))
N_CLASSES = N_EXPERT_GROUPS * len(PAIRS)

TOKEN_TILE = 256
SSM_CHUNK = 16
EXPERT_BLOCK = 256
VMEM_LIMIT = 56 * 1024 * 1024


def _dot(a, b):
    return jnp.dot(a, b, preferred_element_type=F32)


def _cparams(sem):
    return pltpu.CompilerParams(dimension_semantics=sem, vmem_limit_bytes=VMEM_LIMIT)


def _ada_kernel(c_ref, w_ref, b_ref, o_ref):
    c = c_ref[...]
    s = c * jax.nn.sigmoid(c)
    o_ref[0] = jnp.dot(s, w_ref[0], precision=HIGHEST, preferred_element_type=F32) + b_ref[0]


def ada_modulation(c_all, ada_w, ada_b):
    depth, d, n = ada_w.shape
    r = c_all.shape[0]
    tn = n // 4
    return pl.pallas_call(
        _ada_kernel,
        grid=(depth, n // tn),
        in_specs=[
            pl.BlockSpec((r, d), lambda l, j: (0, 0)),
            pl.BlockSpec((1, d, tn), lambda l, j: (l, 0, j)),
            pl.BlockSpec((1, 1, tn), lambda l, j: (l, 0, j)),
        ],
        out_specs=pl.BlockSpec((1, r, tn), lambda l, j: (l, 0, j)),
        out_shape=jax.ShapeDtypeStruct((depth, r, n), F32),
        compiler_params=_cparams(("arbitrary", "arbitrary")),
        name="ada_modulation",
    )(c_all, ada_w, ada_b.reshape(depth, 1, n))


def _inproj_kernel(x_ref, mod_ref, g_ref, w_ref, cos_ref, sa_ref, sb_ref,
                   u_ref, q_ref, k_ref, v_ref, gate_ref, *, d_ssm, d_qk, d_v):
    x = x_ref[0]
    m = mod_ref[0]
    ms = jnp.mean(x * x, axis=-1, keepdims=True)
    y = x * lax.rsqrt(ms + NORM_EPS)
    h = (y * g_ref[...]) * (1.0 + m[1:2]) + m[0:1]
    hb = h.astype(BF16)

    cos = cos_ref[...]
    sa = sa_ref[...]
    sb = sb_ref[...]
    half = ROPE_AXIS_FREQS

    def rope(t):
        return t * cos + pltpu.roll(t, d_qk - half, 1) * sa + pltpu.roll(t, half, 1) * sb

    o0 = 0
    u = _dot(hb, w_ref[:, o0:o0 + d_ssm]).astype(BF16)
    c = SSM_CHUNK
    for ci in range(u.shape[0] // c):
        u_ref[ci, 0] = u[ci * c:(ci + 1) * c, :]
    o0 += d_ssm
    q_ref[0] = (rope(_dot(hb, w_ref[:, o0:o0 + d_qk])) * (QK_HEAD_DIM ** -0.5 * math.log2(math.e))).astype(BF16)
    o0 += d_qk
    k_ref[0] = rope(_dot(hb, w_ref[:, o0:o0 + d_qk])).astype(BF16)
    o0 += d_qk
    v_ref[0] = _dot(hb, w_ref[:, o0:o0 + d_v]).astype(BF16)
    o0 += d_v
    n_gate = gate_ref.shape[-1]
    gate_ref[0] = jax.nn.sigmoid(_dot(hb, w_ref[:, o0:o0 + n_gate])).astype(BF16)


def in_projection(xx, mods, g, w_all, layer, cos_t, sa_t, sb_t, n_ctx_tiles):
    b, lt, d = xx.shape
    nj = lt // TOKEN_TILE
    n_lat = b
    d_ssm = d // 2
    d_qk = N_HEADS * 2 * QK_HEAD_DIM
    d_v = N_HEADS * V_HEAD_DIM
    n_gate = 2 * d
    tm = TOKEN_TILE

    def mod_idx(i, j):
        return (jnp.where(j < n_ctx_tiles, n_lat, i), 0, 0)

    tok = lambda i, j: (i, j, 0)
    kern = functools.partial(_inproj_kernel, d_ssm=d_ssm, d_qk=d_qk, d_v=d_v)
    return pl.pallas_call(
        kern,
        grid=(b, nj),
        in_specs=[
            pl.BlockSpec((1, tm, d), tok),
            pl.BlockSpec((1, 6, d), mod_idx),
            pl.BlockSpec((1, d), lambda i, j: (0, 0)),
            pl.BlockSpec((None,) + w_all.shape[1:], lambda i, j: (layer, 0, 0)),
            pl.BlockSpec((tm, d_qk), lambda i, j: (j, 0)),
            pl.BlockSpec((tm, d_qk), lambda i, j: (j, 0)),
            pl.BlockSpec((tm, d_qk), lambda i, j: (j, 0)),
        ],
        out_specs=[
            pl.BlockSpec((tm // SSM_CHUNK, 1, SSM_CHUNK, d_ssm), lambda i, j: (j, i, 0, 0)),
            pl.BlockSpec((1, tm, d_qk), tok),
            pl.BlockSpec((1, tm, d_qk), tok),
            pl.BlockSpec((1, tm, d_v), tok),
            pl.BlockSpec((1, tm, n_gate), tok),
        ],
        out_shape=[
            jax.ShapeDtypeStruct((lt // SSM_CHUNK, b, SSM_CHUNK, d_ssm), BF16),
            jax.ShapeDtypeStruct((b, lt, d_qk), BF16),
            jax.ShapeDtypeStruct((b, lt, d_qk), BF16),
            jax.ShapeDtypeStruct((b, lt, d_v), BF16),
            jax.ShapeDtypeStruct((b, lt, n_gate), BF16),
        ],
        compiler_params=_cparams(("arbitrary", "arbitrary")),
        name="in_projection",
    )(xx, mods, g.reshape(1, d), w_all, cos_t, sa_t, sb_t)


def rope_tables(lc, s):
    rows = s // GRID_W
    row = jnp.repeat(jnp.arange(rows), GRID_W).astype(F32)
    col = jnp.tile(jnp.arange(GRID_W), rows).astype(F32)
    inv = ROPE_BASE ** (-jnp.arange(ROPE_AXIS_FREQS, dtype=F32) / ROPE_AXIS_FREQS)
    ang_r = row[:, None] * inv
    ang_c = col[:, None] * inv
    ang = jnp.concatenate([ang_r, ang_r, ang_c, ang_c], axis=-1)
    cos = jnp.concatenate([jnp.ones((lc, QK_HEAD_DIM), F32), jnp.cos(ang)], axis=0)
    sin = jnp.concatenate([jnp.zeros((lc, QK_HEAD_DIM), F32), jnp.sin(ang)], axis=0)
    first_half = (jnp.arange(QK_HEAD_DIM) % (2 * ROPE_AXIS_FREQS)) < ROPE_AXIS_FREQS
    sa = jnp.where(first_half, -sin, 0.0)
    sb = jnp.where(first_half, 0.0, sin)
    reps = N_HEADS * 2
    return jnp.tile(cos, (1, reps)), jnp.tile(sa, (1, reps)), jnp.tile(sb, (1, reps))


def _cmul(ar, ai, br, bi):
    return ar * br - ai * bi, ar * bi + ai * br


def s5_chunk_weights(lam_re, lam_im, log_dt, b_re, b_im, c_re, c_im, d_skip):
    c = SSM_CHUNK
    lr = lam_re.astype(F32)
    li = lam_im.astype(F32)
    dt = jnp.exp(log_dt.astype(F32))[..., None]
    decay = jnp.exp(lr * dt)
    ab_re = decay * jnp.cos(li * dt)
    ab_im = decay * jnp.sin(li * dt)
    den = lr * lr + li * li
    nr = ab_re - 1.0
    ni = ab_im
    cr = ((nr * lr + ni * li) / den)[..., None]
    ci = ((ni * lr - nr * li) / den)[..., None]
    br = b_re.astype(F32)
    bi = b_im.astype(F32)
    bb_re = cr * br - ci * bi
    bb_im = cr * bi + ci * br
    ccr = c_re.astype(F32)
    cci = c_im.astype(F32)

    pr = [jnp.ones_like(ab_re)]
    pi = [jnp.zeros_like(ab_im)]
    for _ in range(c):
        r, i = _cmul(pr[-1], pi[-1], ab_re, ab_im)
        pr.append(r)
        pi.append(i)
    pw_re = jnp.stack(pr)
    pw_im = jnp.stack(pi)

    pb_re, pb_im = _cmul(pw_re[..., None], pw_im[..., None], bb_re[None], bb_im[None])
    cp_re, cp_im = _cmul(ccr[None], cci[None], pw_re[:, :, :, None, :], pw_im[:, :, :, None, :])
    kk = (jnp.einsum('dgop,tdgpi->tdgoi', ccr, pb_re[:c], precision=HIGHEST)
          - jnp.einsum('dgop,tdgpi->tdgoi', cci, pb_im[:c], precision=HIGHEST))

    g = lr.shape[1]
    h = SSM_GROUP
    p = SSM_STATE
    ii = jnp.arange(c)[:, None]
    jj = jnp.arange(c)[None, :]
    lag_f = jnp.clip(jj - ii, 0, c - 1)
    lag_b = jnp.clip(ii - jj, 0, c - 1)
    kf = kk[:, 0]
    kb = kk[:, 1]
    tf = jnp.where((jj >= ii)[:, :, None, None, None], kf[lag_f], 0.0)
    tb = jnp.where((ii >= jj)[:, :, None, None, None], kb[lag_b], 0.0)
    tt = jnp.transpose(tf + tb, (2, 0, 4, 1, 3))
    dd = d_skip.astype(F32).reshape(g, h)
    eye_t = jnp.eye(c, dtype=F32)
    eye_h = jnp.eye(h, dtype=F32)
    tt = tt + (eye_t[None, :, None, :, None] * eye_h[None, None, :, None, :]
               * dd[:, None, :, None, None])
    tt = tt.reshape(g, c * h, c * h)

    def s_cols(pbx, d, rev):
        m = pbx[:c, d]
        if rev:
            m = m[::-1]
        return jnp.transpose(m, (1, 0, 3, 2)).reshape(g, c * h, p)

    w1 = jnp.concatenate([
        tt,
        s_cols(pb_re, 0, True), s_cols(pb_re, 1, False),
        s_cols(pb_im, 0, True), s_cols(pb_im, 1, False)], axis=-1)

    def o_rows(cpx, d, sign, fwd):
        m = cpx[1:c + 1, d]
        if not fwd:
            m = m[::-1]
        return sign * jnp.transpose(m, (1, 3, 0, 2)).reshape(g, p, c * h)

    w2 = jnp.concatenate([
        o_rows(cp_re, 0, 1.0, True), o_rows(cp_re, 1, 1.0, False),
        o_rows(cp_im, 0, -1.0, True), o_rows(cp_im, 1, -1.0, False)], axis=1)

    a16 = jnp.zeros((g, 8, 2 * p), F32)
    a16 = a16.at[:, 0, :].set(jnp.concatenate([pw_re[c, 0], pw_re[c, 1]], axis=-1))
    a16 = a16.at[:, 1, :].set(jnp.concatenate([pw_im[c, 0], pw_im[c, 1]], axis=-1))
    return w1.astype(BF16), w2.astype(BF16), a16


def _s5_kernel(x_ref, w1_ref, w2_ref, a_ref, y_ref, z_ref, hin_ref, *, nb, nc_ctx, nc_lat):
    ch = x_ref.shape[-1]
    p = SSM_STATE
    z_ref[...] = _dot(x_ref[0], w1_ref[0])
    a_re = a_ref[0, 0:1, :]
    a_im = a_ref[0, 1:2, :]
    fwd_lane = lax.broadcasted_iota(jnp.int32, (nb, 2 * p), 1) < p

    def phase(base, n, carry):
        def step(i, hc):
            h_re, h_im = hc
            rf = pl.multiple_of((base + i) * nb, nb)
            rb = pl.multiple_of((base + n - 1 - i) * nb, nb)
            hin_ref[pl.ds(rf, nb), 0:p] = h_re[:, 0:p]
            hin_ref[pl.ds(rb, nb), p:2 * p] = h_re[:, p:2 * p]
            hin_ref[pl.ds(rf, nb), 2 * p:3 * p] = h_im[:, 0:p]
            hin_ref[pl.ds(rb, nb), 3 * p:4 * p] = h_im[:, p:2 * p]
            s_re = jnp.where(fwd_lane, z_ref[pl.ds(rf, nb), ch:ch + 2 * p],
                             z_ref[pl.ds(rb, nb), ch:ch + 2 * p])
            s_im = jnp.where(fwd_lane, z_ref[pl.ds(rf, nb), ch + 2 * p:ch + 4 * p],
                             z_ref[pl.ds(rb, nb), ch + 2 * p:ch + 4 * p])
            n_re = a_re * h_re - a_im * h_im + s_re
            n_im = a_re * h_im + a_im * h_re + s_im
            return n_re, n_im
        return lax.fori_loop(0, n, step, carry)

    zero = jnp.zeros((nb, 2 * p), F32)
    hc = phase(0, nc_ctx, (zero, zero))
    phase(nc_ctx, nc_lat, hc)
    y = z_ref[:, 0:ch] + _dot(hin_ref[...].astype(BF16), w2_ref[0])
    y_ref[0] = jax.nn.gelu(y).astype(BF16)


def _regroup_kernel(x_ref, o_ref, *, n_groups):
    h = SSM_GROUP
    for t in range(SSM_CHUNK):
        for g in range(n_groups):
            src = (t * n_groups + g) * h
            o_ref[g, :, t * h:(t + 1) * h] = x_ref[:, src:src + h]


def _ungroup_kernel(y_ref, o_ref, *, n_groups):
    h = SSM_GROUP
    for t in range(SSM_CHUNK):
        for g in range(n_groups):
            dst = (t * n_groups + g) * h
            o_ref[:, dst:dst + h] = y_ref[g, :, t * h:(t + 1) * h]


REGROUP_ROWS = 128


def _regroup(x2, n_groups, inverse):
    c = SSM_CHUNK
    h = SSM_GROUP
    if inverse:
        g, m, _ = x2.shape
    else:
        m = x2.shape[0]
        g = n_groups
    rb = min(REGROUP_ROWS, m)
    assert m % rb == 0
    wide = pl.BlockSpec((rb, c * g * h), lambda i: (i, 0))
    grouped = pl.BlockSpec((g, rb, c * h), lambda i: (0, i, 0))
    kern = functools.partial(_ungroup_kernel if inverse else _regroup_kernel, n_groups=g)
    return pl.pallas_call(
        kern,
        grid=(m // rb,),
        in_specs=[grouped if inverse else wide],
        out_specs=wide if inverse else grouped,
        out_shape=jax.ShapeDtypeStruct((m, c * g * h) if inverse else (g, m, c * h), x2.dtype),
        compiler_params=_cparams(("arbitrary",)),
        name="s5_ungroup" if inverse else "s5_regroup",
    )(x2)


def s5_mixer(u4, w1, w2, a16, lc):
    nc, b, c, dssm = u4.shape
    h = SSM_GROUP
    g = dssm // h
    lt = nc * c
    m = nc * b
    ug = _regroup(u4.reshape(m, c * dssm), g, False)
    kern = functools.partial(_s5_kernel, nb=b, nc_ctx=lc // c, nc_lat=(lt - lc) // c)
    yg = pl.pallas_call(
        kern,
        grid=(g,),
        in_specs=[
            pl.BlockSpec((1, m, c * h), lambda i: (i, 0, 0)),
            pl.BlockSpec((1,) + w1.shape[1:], lambda i: (i, 0, 0)),
            pl.BlockSpec((1,) + w2.shape[1:], lambda i: (i, 0, 0)),
            pl.BlockSpec((1, 8, 2 * SSM_STATE), lambda i: (i, 0, 0)),
        ],
        out_specs=pl.BlockSpec((1, m, c * h), lambda i: (i, 0, 0)),
        out_shape=jax.ShapeDtypeStruct((g, m, c * h), BF16),
        scratch_shapes=[pltpu.VMEM((m, w1.shape[-1]), F32), pltpu.VMEM((m, 4 * SSM_STATE), F32)],
        compiler_params=_cparams(("arbitrary",)),
        name="s5_mixer",
    )(ug, w1, w2, a16)
    return _regroup(yg, g, True).reshape(nc, b, c, dssm)


def _attn_kernel(lam_ref, q_ref, k_ref, v_ref, g_ref, o_ref, *, n_ctx_tiles, lc, out_scale):
    j = pl.program_id(2)
    lam = lam_ref[0]
    q = q_ref[0]
    lo = lax.broadcasted_iota(jnp.int32, q.shape, 1) < QK_HEAD_DIM
    zq = jnp.zeros_like(q)
    q1 = jnp.where(lo, q, zq)
    q2 = jnp.where(lo, zq, q)
    nt = (((1,), (1,)), ((), ()))

    def run(nk):
        k = k_ref[0, 0:nk, :]
        v = v_ref[0, 0:nk, :]
        s1 = lax.dot_general(q1, k, nt, preferred_element_type=F32)
        s2 = lax.dot_general(q2, k, nt, preferred_element_type=F32)
        e1 = jnp.exp2(s1 - jnp.max(s1, axis=-1, keepdims=True))
        e2 = jnp.exp2(s2 - jnp.max(s2, axis=-1, keepdims=True))
        l1 = jnp.sum(e1, axis=-1, keepdims=True)
        l2 = jnp.sum(e2, axis=-1, keepdims=True)
        a = e1 - e2 * (lam * l1 / l2)
        o = _dot(a.astype(BF16), v) * (1.0 / l1)
        ms = jnp.mean(o * o, axis=-1, keepdims=True)
        o = (o * lax.rsqrt(ms + SUBLN_EPS)) * g_ref[...] * out_scale
        o_ref[0] = o.astype(BF16)

    @pl.when(j < n_ctx_tiles)
    def _():
        run(lc)

    @pl.when(j >= n_ctx_tiles)
    def _():
        run(k_ref.shape[1])


def diff_attention(q, k, v, lam, subln_g, lc, lam_init):
    b, lt, _ = q.shape
    tm = TOKEN_TILE
    nj = lt // tm
    hd = 2 * QK_HEAD_DIM
    kern = functools.partial(_attn_kernel, n_ctx_tiles=lc // tm, lc=lc, out_scale=1.0 - lam_init)
    return pl.pallas_call(
        kern,
        grid=(b, N_HEADS, nj),
        in_specs=[
            pl.BlockSpec(memory_space=pltpu.SMEM),
            pl.BlockSpec((1, tm, hd), lambda i, h, j: (i, j, h)),
            pl.BlockSpec((1, lt, hd), lambda i, h, j: (i, 0, h)),
            pl.BlockSpec((1, lt, V_HEAD_DIM), lambda i, h, j: (i, 0, h)),
            pl.BlockSpec((1, V_HEAD_DIM), lambda i, h, j: (0, 0)),
        ],
        out_specs=pl.BlockSpec((1, tm, V_HEAD_DIM), lambda i, h, j: (i, j, h)),
        out_shape=jax.ShapeDtypeStruct((b, lt, N_HEADS * V_HEAD_DIM), BF16),
        compiler_params=_cparams(("arbitrary", "arbitrary", "arbitrary")),
        name="diff_attention",
    )(lam.reshape(1), q, k, v, subln_g.reshape(1, V_HEAD_DIM))


def _route(logits_t, rb):
    aff = jax.nn.sigmoid(logits_t)
    sel = aff + rb
    s = [sel[e:e + 1] for e in range(N_EXPERTS)]
    npg = EXPERTS_PER_GROUP
    gscore = []
    for g in range(N_EXPERT_GROUPS):
        best = None
        for (i, j) in PAIRS:
            ps = s[g * npg + i] + s[g * npg + j]
            best = ps if best is None else jnp.maximum(best, ps)
        gscore.append(best)
    g_sel = jnp.zeros_like(gscore[0], dtype=jnp.int32)
    g_best = gscore[0]
    for g in range(1, N_EXPERT_GROUPS):
        take = gscore[g] > g_best
        g_sel = jnp.where(take, g, g_sel)
        g_best = jnp.where(take, gscore[g], g_best)

    def pick(rows, i):
        out = rows[i]
        for g in range(1, N_EXPERT_GROUPS):
            out = jnp.where(g_sel == g, rows[g * npg + i], out)
        return out

    xs = [pick(s, i) for i in range(npg)]
    i1 = jnp.zeros_like(g_sel)
    v1 = xs[0]
    for i in range(1, npg):
        take = xs[i] > v1
        i1 = jnp.where(take, i, i1)
        v1 = jnp.where(take, xs[i], v1)
    i2 = jnp.full_like(g_sel, -1)
    v2 = jnp.full_like(v1, -jnp.inf)
    for i in range(npg):
        take = (i1 != i) & ((i2 < 0) | (xs[i] > v2))
        i2 = jnp.where(take, i, i2)
        v2 = jnp.where(take, xs[i], v2)
    lo = jnp.minimum(i1, i2)
    hi = jnp.maximum(i1, i2)
    pair = jnp.zeros_like(lo)
    for pi_, (i, j) in enumerate(PAIRS):
        pair = jnp.where((lo == i) & (hi == j), pi_, pair)
    return g_sel * len(PAIRS) + pair


def _merge_kernel(x_ref, yg_ref, o_ref, gate_ref, mod_ref, g2_ref, wglu_ref, wat_ref, wo_ref,
                  rw_ref, rb_ref, xo_ref, h2_ref, route_ref):
    d = x_ref.shape[-1]
    tm = x_ref.shape[1]
    m = mod_ref[0]
    yg = jnp.concatenate([yg_ref[ci, 0] for ci in range(yg_ref.shape[0])], axis=0)
    ab = _dot(yg, wglu_ref[...])
    p_s = ab[:, 0:d] * jax.nn.sigmoid(ab[:, d:2 * d])
    p_a = _dot(o_ref[0], wat_ref[...])
    gate = gate_ref[0]
    mix = gate[:, 0:d].astype(F32) * p_s + gate[:, d:2 * d].astype(F32) * p_a
    x = x_ref[0] + m[2:3] * _dot(mix.astype(BF16), wo_ref[...])
    xo_ref[0] = x
    ms = jnp.mean(x * x, axis=-1, keepdims=True)
    y = x * lax.rsqrt(ms + NORM_EPS)
    h2 = (y * g2_ref[...]) * (1.0 + m[4:5]) + m[3:4]
    lanes = 128
    for kk in range(d // lanes):
        h2_ref[pl.ds(kk, tm, stride=d // lanes), :] = h2[:, kk * lanes:(kk + 1) * lanes]
    lt = _router_logits(h2, rw_ref).T[0:N_EXPERTS]
    cls = _route(lt, rb_ref[...])
    route_ref[...] = jnp.concatenate([cls.astype(F32), jnp.zeros((7,) + cls.shape[1:], F32)], axis=0)


def merge_and_route(xx, yg, o, gate, mods, g2, wglu, wat, wo, layer, rw_pad, rb, n_ctx_tiles):
    b, lt, d = xx.shape
    tm = TOKEN_TILE
    nj = lt // tm
    t = b * lt
    tok = lambda i, j: (i, j, 0)
    const = lambda i, j: (0, 0)

    def mod_idx(i, j):
        return (jnp.where(j < n_ctx_tiles, b, i), 0, 0)

    return pl.pallas_call(
        _merge_kernel,
        grid=(b, nj),
        in_specs=[
            pl.BlockSpec((1, tm, d), tok),
            pl.BlockSpec((tm // SSM_CHUNK, 1, SSM_CHUNK, yg.shape[-1]), lambda i, j: (j, i, 0, 0)),
            pl.BlockSpec((1, tm, o.shape[-1]), tok),
            pl.BlockSpec((1, tm, gate.shape[-1]), tok),
            pl.BlockSpec((1, 6, d), mod_idx),
            pl.BlockSpec((1, d), const),
            pl.BlockSpec((None,) + wglu.shape[1:], lambda i, j: (layer, 0, 0)),
            pl.BlockSpec((None,) + wat.shape[1:], lambda i, j: (layer, 0, 0)),
            pl.BlockSpec((None,) + wo.shape[1:], lambda i, j: (layer, 0, 0)),
            pl.BlockSpec(rw_pad.shape, const),
            pl.BlockSpec(rb.shape, const),
        ],
        out_specs=[
            pl.BlockSpec((1, tm, d), tok),
            pl.BlockSpec((tm * (d // 128), 128), lambda i, j: (i * nj + j, 0)),
            pl.BlockSpec((8, tm), lambda i, j: (0, i * nj + j)),
        ],
        out_shape=[
            jax.ShapeDtypeStruct((b, lt, d), F32),
            jax.ShapeDtypeStruct((t * (d // 128), 128), F32),
            jax.ShapeDtypeStruct((8, t), F32),
        ],
        input_output_aliases={0: 0},
        compiler_params=_cparams(("arbitrary", "arbitrary")),
        name="merge_and_route",
    )(xx, yg, o, gate, mods, g2.reshape(1, d), wglu, wat, wo, rw_pad, rb)


def _rank_kernel(route_ref, o_ref, cnt_ref):
    step = pl.program_id(0)
    n = route_ref.shape[1]

    @pl.when(step == 0)
    def _():
        cnt_ref[...] = jnp.zeros_like(cnt_ref)

    cls = route_ref[0:1, :]
    ncl = cnt_ref.shape[0]
    member = lax.broadcasted_iota(jnp.int32, (ncl, n), 0).astype(F32) == cls
    oh = member.astype(BF16)
    earlier = (lax.broadcasted_iota(jnp.int32, (n, n), 0) < lax.broadcasted_iota(jnp.int32, (n, n), 1)).astype(BF16)
    before = _dot(oh, earlier)
    run = cnt_ref[:, 0:1]
    rank = jnp.sum(jnp.where(member, before + run, 0.0), axis=0, keepdims=True)
    cnt_ref[...] = cnt_ref[...] + jnp.sum(member.astype(F32), axis=1, keepdims=True)
    o_ref[...] = jnp.concatenate([rank, cls, jnp.zeros((6, n), F32)], axis=0).astype(jnp.int32)


def class_ranks(route):
    t = route.shape[1]
    tm = TOKEN_TILE
    return pl.pallas_call(
        _rank_kernel,
        grid=(t // tm,),
        in_specs=[pl.BlockSpec((8, tm), lambda i: (0, i))],
        out_specs=[pl.BlockSpec((8, tm), lambda i: (0, i)), pl.BlockSpec((LANES, LANES), lambda i: (0, 0))],
        out_shape=[jax.ShapeDtypeStruct((8, t), jnp.int32), jax.ShapeDtypeStruct((LANES, LANES), F32)],
        compiler_params=_cparams(("arbitrary",)),
        name="class_ranks",
    )(route)


def block_tables(counts, t):
    r = EXPERT_BLOCK
    ncls = N_CLASSES
    cap = -(-(t + ncls * (r - 1)) // r) * r
    nblk = cap // r
    padded = (counts + r - 1) // r * r
    pend = jnp.cumsum(padded)
    pstart = pend - padded
    n_valid = pend[-1] // r
    blk = jnp.arange(nblk, dtype=jnp.int32)
    blk_cls = jnp.minimum(jnp.sum((pend[None, :] <= (blk * r)[:, None]).astype(jnp.int32), axis=1), ncls - 1)
    last_cls = jnp.max(jnp.where(counts > 0, jnp.arange(ncls, dtype=jnp.int32), 0))
    blk_cls = jnp.where(blk < n_valid, blk_cls, last_cls)
    pair_lo = jnp.array([p[0] for p in PAIRS], jnp.int32)
    pair_hi = jnp.array([p[1] for p in PAIRS], jnp.int32)
    grp = blk_cls // len(PAIRS)
    pr = blk_cls % len(PAIRS)
    e_lo = grp * EXPERTS_PER_GROUP + pair_lo[pr]
    e_hi = grp * EXPERTS_PER_GROUP + pair_hi[pr]
    pstart_pad = jnp.pad(pstart.astype(jnp.int32), (0, 32 - ncls))
    return pstart_pad, e_lo, e_hi, n_valid.astype(jnp.int32).reshape(1), cap


TOKEN_SUBLANES = 8
LANES = 128
GATHER_UNROLL = 8


def _slot_copies(pstart_ref, cls_ref, rank_ref, n, make_copy, sem):
    assert n % GATHER_UNROLL == 0

    def issue(i, carry):
        for uu in range(GATHER_UNROLL):
            r = i * GATHER_UNROLL + uu
            slot = pstart_ref[cls_ref[0, 0, r]] + rank_ref[0, 0, r]
            make_copy(r, slot, sem).start(priority=uu % 2)
        return carry
    lax.fori_loop(0, n // GATHER_UNROLL, issue, 0)


def _tile_rows(ref, first, n):
    ts = TOKEN_SUBLANES
    return ref.at[pl.ds(pl.multiple_of(first * ts, ts), n * ts)]


def _tiles_to_rows(ref, n):
    ts = TOKEN_SUBLANES
    return jnp.concatenate([ref[pl.ds(kk, n, stride=ts), :] for kk in range(ts)], axis=1)


def _dispatch_kernel(pstart_ref, cls_ref, rank_ref, src_ref, zero_ref, dst_ref, sem):
    del zero_ref
    n = cls_ref.shape[-1]
    base = pl.program_id(0) * n

    def make_copy(r, slot, s):
        return pltpu.make_async_copy(_tile_rows(src_ref, base + r, 1), _tile_rows(dst_ref, slot, 1), s)

    _slot_copies(pstart_ref, cls_ref, rank_ref, n, make_copy, sem)
    pltpu.make_async_copy(_tile_rows(src_ref, 0, n), _tile_rows(dst_ref, 0, n), sem).wait()


def dispatch_tokens(h2, pstart, cls_blk, rank_blk, cap):
    ts = TOKEN_SUBLANES
    nblk, _, n = cls_blk.shape
    blk = lambda i, ps: (i, 0, 0)
    grid_spec = pltpu.PrefetchScalarGridSpec(
        num_scalar_prefetch=1,
        grid=(nblk,),
        in_specs=[
            pl.BlockSpec((1, 1, n), blk, memory_space=pltpu.SMEM),
            pl.BlockSpec((1, 1, n), blk, memory_space=pltpu.SMEM),
            pl.BlockSpec(memory_space=pl.ANY),
            pl.BlockSpec(memory_space=pl.ANY),
        ],
        out_specs=pl.BlockSpec(memory_space=pl.ANY),
        scratch_shapes=[pltpu.SemaphoreType.DMA(())],
    )
    return pl.pallas_call(
        _dispatch_kernel,
        grid_spec=grid_spec,
        out_shape=jax.ShapeDtypeStruct((cap * ts, LANES), h2.dtype),
        input_output_aliases={4: 0},
        compiler_params=_cparams(("arbitrary",)),
        name="dispatch_tokens",
    )(pstart, cls_blk, rank_blk, h2, jnp.zeros((cap * ts, LANES), h2.dtype))


def _combine_kernel(pstart_ref, cls_ref, rank_ref, tab_ref, x_ref, mod_ref, o_ref, buf_ref, sem):
    n = x_ref.shape[1]

    def make_copy(r, slot, s):
        return pltpu.make_async_copy(_tile_rows(tab_ref, slot, 1), _tile_rows(buf_ref, r, 1), s)

    _slot_copies(pstart_ref, cls_ref, rank_ref, n, make_copy, sem)
    pltpu.make_async_copy(_tile_rows(tab_ref, 0, n), buf_ref, sem).wait()
    g2 = mod_ref[0][5:6]
    for kk in range(TOKEN_SUBLANES):
        cols = slice(kk * LANES, (kk + 1) * LANES)
        o_ref[0, :, cols] = x_ref[0, :, cols] + g2[:, cols] * buf_ref[pl.ds(kk, n, stride=TOKEN_SUBLANES), :]


def combine_residual(xx, ys, pstart, cls_blk, rank_blk, mods, n_ctx_tiles):
    b, lt, d = xx.shape
    tm = TOKEN_TILE
    nj = lt // tm
    tok = lambda i, j, ps: (i, j, 0)
    blk = lambda i, j, ps: (i * nj + j, 0, 0)

    def mod_idx(i, j, ps):
        return (jnp.where(j < n_ctx_tiles, b, i), 0, 0)

    grid_spec = pltpu.PrefetchScalarGridSpec(
        num_scalar_prefetch=1,
        grid=(b, nj),
        in_specs=[
            pl.BlockSpec((1, 1, tm), blk, memory_space=pltpu.SMEM),
            pl.BlockSpec((1, 1, tm), blk, memory_space=pltpu.SMEM),
            pl.BlockSpec(memory_space=pl.ANY),
            pl.BlockSpec((1, tm, d), tok),
            pl.BlockSpec((1, 6, d), mod_idx),
        ],
        out_specs=pl.BlockSpec((1, tm, d), tok),
        scratch_shapes=[pltpu.VMEM((tm * TOKEN_SUBLANES, LANES), F32), pltpu.SemaphoreType.DMA(())],
    )
    return pl.pallas_call(
        _combine_kernel,
        grid_spec=grid_spec,
        out_shape=jax.ShapeDtypeStruct((b, lt, d), F32),
        input_output_aliases={4: 0},
        compiler_params=_cparams(("arbitrary", "arbitrary")),
        name="combine_residual",
    )(pstart, cls_blk, rank_blk, ys, xx, mods)


def _router_logits(h2, rw_ref):
    n = h2.shape[0]
    hi = h2.astype(BF16)
    lo = (h2 - hi.astype(F32)).astype(BF16)
    pp = _dot(jnp.concatenate([hi, lo], axis=0), rw_ref[...])
    return (pp[0:n, 0:LANES] + pp[0:n, LANES:2 * LANES]) + (pp[n:2 * n, 0:LANES] + pp[n:2 * n, LANES:2 * LANES])


def _expert_kernel(elo_ref, ehi_ref, nv_ref, x_ref, rw_ref, w1a, w3a, w2a, w1b, w3b, w2b, o_ref):
    i = pl.program_id(0)
    n = x_ref.shape[0] // TOKEN_SUBLANES

    @pl.when(i < nv_ref[0])
    def _():
        x = _tiles_to_rows(x_ref, n)
        xb = x.astype(BF16)

        def ffn(w1, w3, w2):
            h1 = _dot(xb, w1[...])
            h3 = _dot(xb, w3[...])
            hid = (h1 * jax.nn.sigmoid(h1)) * h3
            return _dot(hid.astype(BF16), w2[...])

        aff = jax.nn.sigmoid(_router_logits(x, rw_ref))
        lane = lax.broadcasted_iota(jnp.int32, aff.shape, 1)
        a_lo = jnp.sum(jnp.where(lane == elo_ref[i], aff, 0.0), axis=1, keepdims=True)
        a_hi = jnp.sum(jnp.where(lane == ehi_ref[i], aff, 0.0), axis=1, keepdims=True)
        tot = a_lo + a_hi
        y = (a_lo / tot) * ffn(w1a, w3a, w2a) + (a_hi / tot) * ffn(w1b, w3b, w2b)
        for kk in range(TOKEN_SUBLANES):
            o_ref[pl.ds(kk, n, stride=TOKEN_SUBLANES), :] = y[:, kk * LANES:(kk + 1) * LANES]

    @pl.when(i >= nv_ref[0])
    def _():
        o_ref[...] = jnp.zeros_like(o_ref)


def expert_blocks(xs, rw_pad, e_lo, e_hi, n_valid, w1, w3, w2, layer):
    ts = TOKEN_SUBLANES
    cap = xs.shape[0] // ts
    d = ts * LANES
    r = EXPERT_BLOCK
    nblk = cap // r
    de = w1.shape[-1]
    row = lambda i, elo, ehi, nv: (i, 0)
    wlo = lambda i, elo, ehi, nv: (layer, elo[i], 0, 0)
    whi = lambda i, elo, ehi, nv: (layer, ehi[i], 0, 0)
    grid_spec = pltpu.PrefetchScalarGridSpec(
        num_scalar_prefetch=3,
        grid=(nblk,),
        in_specs=[
            pl.BlockSpec((r * ts, LANES), row),
            pl.BlockSpec(rw_pad.shape, lambda i, elo, ehi, nv: (0, 0)),
            pl.BlockSpec((None, None, d, de), wlo),
            pl.BlockSpec((None, None, d, de), wlo),
            pl.BlockSpec((None, None, de, d), wlo),
            pl.BlockSpec((None, None, d, de), whi),
            pl.BlockSpec((None, None, d, de), whi),
            pl.BlockSpec((None, None, de, d), whi),
        ],
        out_specs=pl.BlockSpec((r * ts, LANES), row),
    )
    return pl.pallas_call(
        _expert_kernel,
        grid_spec=grid_spec,
        out_shape=jax.ShapeDtypeStruct((cap * ts, LANES), F32),
        compiler_params=_cparams(("arbitrary",)),
        name="expert_blocks",
    )(e_lo, e_hi, n_valid, xs, rw_pad, w1, w3, w2, w1, w3, w2)


def _final_norm_kernel(x_ref, g_ref, o_ref):
    x = x_ref[0]
    ms = jnp.mean(x * x, axis=-1, keepdims=True)
    o_ref[0] = (x * lax.rsqrt(ms + NORM_EPS)) * g_ref[...]


def final_norm(xx, g, lc):
    b, lt, d = xx.shape
    tm = TOKEN_TILE
    s = lt - lc
    off = lc // tm
    return pl.pallas_call(
        _final_norm_kernel,
        grid=(b, s // tm),
        in_specs=[
            pl.BlockSpec((1, tm, d), lambda i, j: (i, j + off, 0)),
            pl.BlockSpec((1, d), lambda i, j: (0, 0)),
        ],
        out_specs=pl.BlockSpec((1, tm, d), lambda i, j: (i, j, 0)),
        out_shape=jax.ShapeDtypeStruct((b, s, d), F32),
        compiler_params=_cparams(("arbitrary", "arbitrary")),
        name="final_norm",
    )(xx, g.reshape(1, d))


def kernel(x, c, ctx, c_ctx, ada_w, ada_b, norm1_g, norm2_g, final_g, w_in,
           ssm_lam_re, ssm_lam_im, ssm_log_dt, ssm_b_re, ssm_b_im, ssm_c_re, ssm_c_im, ssm_d, w_glu,
           attn_lambda, attn_subln_g, w_attn_out, w_o, router_w, router_b, exp_w1, exp_w3, exp_w2):
    b, s, d = x.shape
    lc = ctx.shape[1]
    depth = w_in.shape[0]
    tm = TOKEN_TILE
    assert lc % tm == 0 and s % tm == 0 and s % GRID_W == 0 and lc % SSM_CHUNK == 0
    n_ctx_tiles = lc // tm
    lt = lc + s

    xx = jnp.concatenate([ctx, x], axis=1)

    n_mod_rows = -(-(b + 1) // 8) * 8
    c_all = jnp.zeros((n_mod_rows, d), F32).at[:b].set(c).at[b].set(c_ctx)
    mods_all = ada_modulation(c_all, ada_w, ada_b).reshape(depth, n_mod_rows, 6, d)

    cos_t, sa_t, sb_t = rope_tables(lc, s)
    assert d == TOKEN_SUBLANES * LANES
    rw_f = jnp.zeros((d, LANES), F32).at[:, :N_EXPERTS].set(router_w.astype(F32))
    rw_hi = rw_f.astype(BF16)
    rw_lo = (rw_f - rw_hi.astype(F32)).astype(BF16)
    rw_pad = jnp.concatenate([rw_hi, rw_lo], axis=1)
    rb = router_b.astype(F32).reshape(N_EXPERTS, 1)

    w_in_bf = w_in.astype(BF16)
    w_glu_bf = w_glu.astype(BF16)
    w_at_bf = w_attn_out.astype(BF16)
    w_o_bf = w_o.astype(BF16)
    e1_bf = exp_w1.astype(BF16)
    e3_bf = exp_w3.astype(BF16)
    e2_bf = exp_w2.astype(BF16)

    for l in range(depth):
        lam_init = 0.8 - 0.6 * math.exp(-0.3 * l)
        mods = mods_all[l]
        u, q, k, v, gate = in_projection(xx, mods, norm1_g[l], w_in_bf, l,
                                         cos_t, sa_t, sb_t, n_ctx_tiles)

        w1s, w2s, a16 = s5_chunk_weights(ssm_lam_re[l], ssm_lam_im[l], ssm_log_dt[l], ssm_b_re[l], ssm_b_im[l],
                                         ssm_c_re[l], ssm_c_im[l], ssm_d[l])
        yg = s5_mixer(u, w1s, w2s, a16, lc)

        lf = attn_lambda[l].astype(F32)
        lam = jnp.exp(jnp.sum(lf[0] * lf[1])) - jnp.exp(jnp.sum(lf[2] * lf[3])) + lam_init
        o = diff_attention(q, k, v, lam, attn_subln_g[l].astype(F32), lc, lam_init)

        xx, h2, route = merge_and_route(xx, yg, o, gate, mods, norm2_g[l], w_glu_bf, w_at_bf, w_o_bf, l,
                                        rw_pad, rb, n_ctx_tiles)

        ranks, hist = class_ranks(route)
        n_tiles = b * lt // tm
        rank_blk = ranks[0].reshape(n_tiles, 1, tm)
        cls_blk = ranks[1].reshape(n_tiles, 1, tm)
        counts = hist[:N_CLASSES, 0].astype(jnp.int32)
        pstart, e_lo, e_hi, n_valid, cap = block_tables(counts, b * lt)
        xs = dispatch_tokens(h2, pstart, cls_blk, rank_blk, cap)
        ys = expert_blocks(xs, rw_pad, e_lo, e_hi, n_valid, e1_bf, e3_bf, e2_bf, l)
        xx = combine_residual(xx, ys, pstart, cls_blk, rank_blk, mods, n_ctx_tiles)

    return final_norm(xx, final_g, lc)
```

```python
import functools
import math

import jax
import jax.numpy as jnp
from jax import lax
from jax.experimental import pallas as pl
from jax.experimental.pallas import tpu as pltpu

F32 = jnp.float32
BF16 = jnp.bfloat16
HIGHEST = lax.Precision.HIGHEST

GRID_W = 64
SSM_GROUP = 16
SSM_STATE = 64
N_HEADS = 4
QK_HEAD_DIM = 64
V_HEAD_DIM = 128
ROPE_BASE = 10000.0
ROPE_AXIS_FREQS = QK_HEAD_DIM // 4
N_EXPERTS = 16
EXPERTS_PER_GROUP = 4
N_EXPERT_GROUPS = N_EXPERTS // EXPERTS_PER_GROUP
NORM_EPS = 1e-6
SUBLN_EPS = 1e-5

PAIRS = ((0, 1), (0, 2), (0, 3), (1, 2), (1, 3), (2, 3))
N_CLASSES = N_EXPERT_GROUPS * len(PAIRS)

TOKEN_TILE = 256
SSM_CHUNK = 16
EXPERT_BLOCK = 256
VMEM_LIMIT = 56 * 1024 * 1024


def _dot(a, b):
    return jnp.dot(a, b, preferred_element_type=F32)


def _cparams(sem):
    return pltpu.CompilerParams(dimension_semantics=sem, vmem_limit_bytes=VMEM_LIMIT)


def _ada_kernel(c_ref, w_ref, b_ref, o_ref):
    c = c_ref[...]
    s = c * jax.nn.sigmoid(c)
    o_ref[0] = jnp.dot(s, w_ref[0], precision=HIGHEST, preferred_element_type=F32) + b_ref[0]


def ada_modulation(c_all, ada_w, ada_b):
    depth, d, n = ada_w.shape
    r = c_all.shape[0]
    tn = n // 4
    return pl.pallas_call(
        _ada_kernel,
        grid=(depth, n // tn),
        in_specs=[
            pl.BlockSpec((r, d), lambda l, j: (0, 0)),
            pl.BlockSpec((1, d, tn), lambda l, j: (l, 0, j)),
            pl.BlockSpec((1, 1, tn), lambda l, j: (l, 0, j)),
        ],
        out_specs=pl.BlockSpec((1, r, tn), lambda l, j: (l, 0, j)),
        out_shape=jax.ShapeDtypeStruct((depth, r, n), F32),
        compiler_params=_cparams(("arbitrary", "arbitrary")),
        name="ada_modulation",
    )(c_all, ada_w, ada_b.reshape(depth, 1, n))


def _inproj_kernel(x_ref, mod_ref, g_ref, w_ref, cos_ref, sa_ref, sb_ref,
                   u_ref, q_ref, k_ref, v_ref, gate_ref, *, d_ssm, d_qk, d_v):
    x = x_ref[0]
    m = mod_ref[0]
    ms = jnp.mean(x * x, axis=-1, keepdims=True)
    y = x * lax.rsqrt(ms + NORM_EPS)
    h = (y * g_ref[...]) * (1.0 + m[1:2]) + m[0:1]
    hb = h.astype(BF16)

    cos = cos_ref[...]
    sa = sa_ref[...]
    sb = sb_ref[...]
    half = ROPE_AXIS_FREQS

    def rope(t):
        return t * cos + pltpu.roll(t, d_qk - half, 1) * sa + pltpu.roll(t, half, 1) * sb

    o0 = 0
    u = _dot(hb, w_ref[:, o0:o0 + d_ssm]).astype(BF16)
    c = SSM_CHUNK
    for ci in range(u.shape[0] // c):
        u_ref[ci, 0] = u[ci * c:(ci + 1) * c, :]
    o0 += d_ssm
    q_ref[0] = (rope(_dot(hb, w_ref[:, o0:o0 + d_qk])) * (QK_HEAD_DIM ** -0.5 * math.log2(math.e))).astype(BF16)
    o0 += d_qk
    k_ref[0] = rope(_dot(hb, w_ref[:, o0:o0 + d_qk])).astype(BF16)
    o0 += d_qk
    v_ref[0] = _dot(hb, w_ref[:, o0:o0 + d_v]).astype(BF16)
    o0 += d_v
    n_gate = gate_ref.shape[-1]
    gate_ref[0] = jax.nn.sigmoid(_dot(hb, w_ref[:, o0:o0 + n_gate])).astype(BF16)


def in_projection(xx, mods, g, w_all, layer, cos_t, sa_t, sb_t, n_ctx_tiles):
    b, lt, d = xx.shape
    nj = lt // TOKEN_TILE
    n_lat = b
    d_ssm = d // 2
    d_qk = N_HEADS * 2 * QK_HEAD_DIM
    d_v = N_HEADS * V_HEAD_DIM
    n_gate = 2 * d
    tm = TOKEN_TILE

    def mod_idx(i, j):
        return (jnp.where(j < n_ctx_tiles, n_lat, i), 0, 0)

    tok = lambda i, j: (i, j, 0)
    kern = functools.partial(_inproj_kernel, d_ssm=d_ssm, d_qk=d_qk, d_v=d_v)
    return pl.pallas_call(
        kern,
        grid=(b, nj),
        in_specs=[
            pl.BlockSpec((1, tm, d), tok),
            pl.BlockSpec((1, 6, d), mod_idx),
            pl.BlockSpec((1, d), lambda i, j: (0, 0)),
            pl.BlockSpec((None,) + w_all.shape[1:], lambda i, j: (layer, 0, 0)),
            pl.BlockSpec((tm, d_qk), lambda i, j: (j, 0)),
            pl.BlockSpec((tm, d_qk), lambda i, j: (j, 0)),
            pl.BlockSpec((tm, d_qk), lambda i, j: (j, 0)),
        ],
        out_specs=[
            pl.BlockSpec((tm // SSM_CHUNK, 1, SSM_CHUNK, d_ssm), lambda i, j: (j, i, 0, 0)),
            pl.BlockSpec((1, tm, d_qk), tok),
            pl.BlockSpec((1, tm, d_qk), tok),
            pl.BlockSpec((1, tm, d_v), tok),
            pl.BlockSpec((1, tm, n_gate), tok),
        ],
        out_shape=[
            jax.ShapeDtypeStruct((lt // SSM_CHUNK, b, SSM_CHUNK, d_ssm), BF16),
            jax.ShapeDtypeStruct((b, lt, d_qk), BF16),
            jax.ShapeDtypeStruct((b, lt, d_qk), BF16),
            jax.ShapeDtypeStruct((b, lt, d_v), BF16),
            jax.ShapeDtypeStruct((b, lt, n_gate), BF16),
        ],
        compiler_params=_cparams(("arbitrary", "arbitrary")),
        name="in_projection",
    )(xx, mods, g.reshape(1, d), w_all, cos_t, sa_t, sb_t)


def rope_tables(lc, s):
    rows = s // GRID_W
    row = jnp.repeat(jnp.arange(rows), GRID_W).astype(F32)
    col = jnp.tile(jnp.arange(GRID_W), rows).astype(F32)
    inv = ROPE_BASE ** (-jnp.arange(ROPE_AXIS_FREQS, dtype=F32) / ROPE_AXIS_FREQS)
    ang_r = row[:, None] * inv
    ang_c = col[:, None] * inv
    ang = jnp.concatenate([ang_r, ang_r, ang_c, ang_c], axis=-1)
    cos = jnp.concatenate([jnp.ones((lc, QK_HEAD_DIM), F32), jnp.cos(ang)], axis=0)
    sin = jnp.concatenate([jnp.zeros((lc, QK_HEAD_DIM), F32), jnp.sin(ang)], axis=0)
    first_half = (jnp.arange(QK_HEAD_DIM) % (2 * ROPE_AXIS_FREQS)) < ROPE_AXIS_FREQS
    sa = jnp.where(first_half, -sin, 0.0)
    sb = jnp.where(first_half, 0.0, sin)
    reps = N_HEADS * 2
    return jnp.tile(cos, (1, reps)), jnp.tile(sa, (1, reps)), jnp.tile(sb, (1, reps))


def _cmul(ar, ai, br, bi):
    return ar * br - ai * bi, ar * bi + ai * br


def s5_chunk_weights(lam_re, lam_im, log_dt, b_re, b_im, c_re, c_im, d_skip):
    c = SSM_CHUNK
    lr = lam_re.astype(F32)
    li = lam_im.astype(F32)
    dt = jnp.exp(log_dt.astype(F32))[..., None]
    decay = jnp.exp(lr * dt)
    ab_re = decay * jnp.cos(li * dt)
    ab_im = decay * jnp.sin(li * dt)
    den = lr * lr + li * li
    nr = ab_re - 1.0
    ni = ab_im
    cr = ((nr * lr + ni * li) / den)[..., None]
    ci = ((ni * lr - nr * li) / den)[..., None]
    br = b_re.astype(F32)
    bi = b_im.astype(F32)
    bb_re = cr * br - ci * bi
    bb_im = cr * bi + ci * br
    ccr = c_re.astype(F32)
    cci = c_im.astype(F32)

    pr = [jnp.ones_like(ab_re)]
    pi = [jnp.zeros_like(ab_im)]
    for _ in range(c):
        r, i = _cmul(pr[-1], pi[-1], ab_re, ab_im)
        pr.append(r)
        pi.append(i)
    pw_re = jnp.stack(pr)
    pw_im = jnp.stack(pi)

    pb_re, pb_im = _cmul(pw_re[..., None], pw_im[..., None], bb_re[None], bb_im[None])
    cp_re, cp_im = _cmul(ccr[None], cci[None], pw_re[:, :, :, None, :], pw_im[:, :, :, None, :])
    kk = (jnp.einsum('dgop,tdgpi->tdgoi', ccr, pb_re[:c], precision=HIGHEST)
          - jnp.einsum('dgop,tdgpi->tdgoi', cci, pb_im[:c], precision=HIGHEST))

    g = lr.shape[1]
    h = SSM_GROUP
    p = SSM_STATE
    ii = jnp.arange(c)[:, None]
    jj = jnp.arange(c)[None, :]
    lags = jnp.arange(c)[:, None, None]
    shift_f = ((jj - ii)[None] == lags).astype(F32)
    shift_b = ((ii - jj)[None] == lags).astype(F32)
    tt = (jnp.einsum('tij,tgon->ginjo', shift_f, kk[:, 0], precision=HIGHEST)
          + jnp.einsum('tij,tgon->ginjo', shift_b, kk[:, 1], precision=HIGHEST))
    dd = d_skip.astype(F32).reshape(g, h)
    eye_t = jnp.eye(c, dtype=F32)
    eye_h = jnp.eye(h, dtype=F32)
    tt = tt + (eye_t[None, :, None, :, None] * eye_h[None, None, :, None, :]
               * dd[:, None, :, None, None])
    tt = tt.reshape(g, c * h, c * h)

    def s_cols(pbx, d, rev):
        m = pbx[:c, d]
        if rev:
            m = m[::-1]
        return jnp.transpose(m, (1, 0, 3, 2)).reshape(g, c * h, p)

    w1 = jnp.concatenate([
        tt,
        s_cols(pb_re, 0, True), s_cols(pb_re, 1, False),
        s_cols(pb_im, 0, True), s_cols(pb_im, 1, False)], axis=-1)

    def o_rows(cpx, d, sign, fwd):
        m = cpx[1:c + 1, d]
        if not fwd:
            m = m[::-1]
        return sign * jnp.transpose(m, (1, 3, 0, 2)).reshape(g, p, c * h)

    w2 = jnp.concatenate([
        o_rows(cp_re, 0, 1.0, True), o_rows(cp_re, 1, 1.0, False),
        o_rows(cp_im, 0, -1.0, True), o_rows(cp_im, 1, -1.0, False)], axis=1)

    a16 = jnp.zeros((g, 8, 2 * p), F32)
    a16 = a16.at[:, 0, :].set(jnp.concatenate([pw_re[c, 0], pw_re[c, 1]], axis=-1))
    a16 = a16.at[:, 1, :].set(jnp.concatenate([pw_im[c, 0], pw_im[c, 1]], axis=-1))
    return w1.astype(BF16), w2.astype(BF16), a16


def _s5_kernel(x_ref, w1_ref, w2_ref, a_ref, y_ref, z_ref, hin_ref, *, nb, nc_ctx, nc_lat):
    ch = x_ref.shape[-1]
    p = SSM_STATE
    z_ref[...] = _dot(x_ref[0], w1_ref[0])
    a_re = a_ref[0, 0:1, :]
    a_im = a_ref[0, 1:2, :]
    fwd_lane = lax.broadcasted_iota(jnp.int32, (nb, 2 * p), 1) < p

    def phase(base, n, carry):
        def step(i, hc):
            h_re, h_im = hc
            rf = pl.multiple_of((base + i) * nb, nb)
            rb = pl.multiple_of((base + n - 1 - i) * nb, nb)
            hin_ref[pl.ds(rf, nb), 0:p] = h_re[:, 0:p]
            hin_ref[pl.ds(rb, nb), p:2 * p] = h_re[:, p:2 * p]
            hin_ref[pl.ds(rf, nb), 2 * p:3 * p] = h_im[:, 0:p]
            hin_ref[pl.ds(rb, nb), 3 * p:4 * p] = h_im[:, p:2 * p]
            s_re = jnp.where(fwd_lane, z_ref[pl.ds(rf, nb), ch:ch + 2 * p],
                             z_ref[pl.ds(rb, nb), ch:ch + 2 * p])
            s_im = jnp.where(fwd_lane, z_ref[pl.ds(rf, nb), ch + 2 * p:ch + 4 * p],
                             z_ref[pl.ds(rb, nb), ch + 2 * p:ch + 4 * p])
            n_re = a_re * h_re - a_im * h_im + s_re
            n_im = a_re * h_im + a_im * h_re + s_im
            return n_re, n_im
        return lax.fori_loop(0, n, step, carry)

    zero = jnp.zeros((nb, 2 * p), F32)
    hc = phase(0, nc_ctx, (zero, zero))
    phase(nc_ctx, nc_lat, hc)
    y = z_ref[:, 0:ch] + _dot(hin_ref[...].astype(BF16), w2_ref[0])
    y_ref[0] = jax.nn.gelu(y).astype(BF16)


def _regroup_kernel(x_ref, o_ref, *, n_groups):
    h = SSM_GROUP
    for t in range(SSM_CHUNK):
        for g in range(n_groups):
            src = (t * n_groups + g) * h
            o_ref[g, :, t * h:(t + 1) * h] = x_ref[:, src:src + h]


def _ungroup_kernel(y_ref, o_ref, *, n_groups):
    h = SSM_GROUP
    for t in range(SSM_CHUNK):
        for g in range(n_groups):
            dst = (t * n_groups + g) * h
            o_ref[:, dst:dst + h] = y_ref[g, :, t * h:(t + 1) * h]


REGROUP_ROWS = 128


def _regroup(x2, n_groups, inverse):
    c = SSM_CHUNK
    h = SSM_GROUP
    if inverse:
        g, m, _ = x2.shape
    else:
        m = x2.shape[0]
        g = n_groups
    rb = min(REGROUP_ROWS, m)
    assert m % rb == 0
    wide = pl.BlockSpec((rb, c * g * h), lambda i: (i, 0))
    grouped = pl.BlockSpec((g, rb, c * h), lambda i: (0, i, 0))
    kern = functools.partial(_ungroup_kernel if inverse else _regroup_kernel, n_groups=g)
    return pl.pallas_call(
        kern,
        grid=(m // rb,),
        in_specs=[grouped if inverse else wide],
        out_specs=wide if inverse else grouped,
        out_shape=jax.ShapeDtypeStruct((m, c * g * h) if inverse else (g, m, c * h), x2.dtype),
        compiler_params=_cparams(("arbitrary",)),
        name="s5_ungroup" if inverse else "s5_regroup",
    )(x2)


def s5_mixer(u4, w1, w2, a16, layer, lc):
    nc, b, c, dssm = u4.shape
    h = SSM_GROUP
    g = dssm // h
    lt = nc * c
    m = nc * b
    ug = _regroup(u4.reshape(m, c * dssm), g, False)
    kern = functools.partial(_s5_kernel, nb=b, nc_ctx=lc // c, nc_lat=(lt - lc) // c)
    yg = pl.pallas_call(
        kern,
        grid=(g,),
        in_specs=[
            pl.BlockSpec((1, m, c * h), lambda i: (i, 0, 0)),
            pl.BlockSpec((None, 1) + w1.shape[2:], lambda i: (layer, i, 0, 0)),
            pl.BlockSpec((None, 1) + w2.shape[2:], lambda i: (layer, i, 0, 0)),
            pl.BlockSpec((None, 1, 8, 2 * SSM_STATE), lambda i: (layer, i, 0, 0)),
        ],
        out_specs=pl.BlockSpec((1, m, c * h), lambda i: (i, 0, 0)),
        out_shape=jax.ShapeDtypeStruct((g, m, c * h), BF16),
        scratch_shapes=[pltpu.VMEM((m, w1.shape[-1]), F32), pltpu.VMEM((m, 4 * SSM_STATE), F32)],
        compiler_params=_cparams(("arbitrary",)),
        name="s5_mixer",
    )(ug, w1, w2, a16)
    return _regroup(yg, g, True).reshape(nc, b, c, dssm)


def _attn_kernel(lam_ref, q_ref, k_ref, v_ref, g_ref, o_ref, *, n_ctx_tiles, lc, out_scale):
    j = pl.program_id(2)
    lam = lam_ref[0]
    q = q_ref[0]
    lo = lax.broadcasted_iota(jnp.int32, q.shape, 1) < QK_HEAD_DIM
    zq = jnp.zeros_like(q)
    q1 = jnp.where(lo, q, zq)
    q2 = jnp.where(lo, zq, q)
    nt = (((1,), (1,)), ((), ()))

    def run(nk):
        k = k_ref[0, 0:nk, :]
        v = v_ref[0, 0:nk, :]
        s1 = lax.dot_general(q1, k, nt, preferred_element_type=F32)
        s2 = lax.dot_general(q2, k, nt, preferred_element_type=F32)
        e1 = jnp.exp2(s1 - jnp.max(s1, axis=-1, keepdims=True))
        e2 = jnp.exp2(s2 - jnp.max(s2, axis=-1, keepdims=True))
        l1 = jnp.sum(e1, axis=-1, keepdims=True)
        l2 = jnp.sum(e2, axis=-1, keepdims=True)
        a = e1 - e2 * (lam * l1 / l2)
        o = _dot(a.astype(BF16), v) * (1.0 / l1)
        ms = jnp.mean(o * o, axis=-1, keepdims=True)
        o = (o * lax.rsqrt(ms + SUBLN_EPS)) * g_ref[...] * out_scale
        o_ref[0] = o.astype(BF16)

    @pl.when(j < n_ctx_tiles)
    def _():
        run(lc)

    @pl.when(j >= n_ctx_tiles)
    def _():
        run(k_ref.shape[1])


def diff_attention(q, k, v, lam, subln_g, lc, lam_init):
    b, lt, _ = q.shape
    tm = TOKEN_TILE
    nj = lt // tm
    hd = 2 * QK_HEAD_DIM
    kern = functools.partial(_attn_kernel, n_ctx_tiles=lc // tm, lc=lc, out_scale=1.0 - lam_init)
    return pl.pallas_call(
        kern,
        grid=(b, N_HEADS, nj),
        in_specs=[
            pl.BlockSpec(memory_space=pltpu.SMEM),
            pl.BlockSpec((1, tm, hd), lambda i, h, j: (i, j, h)),
            pl.BlockSpec((1, lt, hd), lambda i, h, j: (i, 0, h)),
            pl.BlockSpec((1, lt, V_HEAD_DIM), lambda i, h, j: (i, 0, h)),
            pl.BlockSpec((1, V_HEAD_DIM), lambda i, h, j: (0, 0)),
        ],
        out_specs=pl.BlockSpec((1, tm, V_HEAD_DIM), lambda i, h, j: (i, j, h)),
        out_shape=jax.ShapeDtypeStruct((b, lt, N_HEADS * V_HEAD_DIM), BF16),
        compiler_params=_cparams(("arbitrary", "arbitrary", "arbitrary")),
        name="diff_attention",
    )(lam.reshape(1), q, k, v, subln_g.reshape(1, V_HEAD_DIM))


def _route(logits_t, rb):
    aff = jax.nn.sigmoid(logits_t)
    sel = aff + rb
    s = [sel[e:e + 1] for e in range(N_EXPERTS)]
    npg = EXPERTS_PER_GROUP
    gscore = []
    for g in range(N_EXPERT_GROUPS):
        best = None
        for (i, j) in PAIRS:
            ps = s[g * npg + i] + s[g * npg + j]
            best = ps if best is None else jnp.maximum(best, ps)
        gscore.append(best)
    g_sel = jnp.zeros_like(gscore[0], dtype=jnp.int32)
    g_best = gscore[0]
    for g in range(1, N_EXPERT_GROUPS):
        take = gscore[g] > g_best
        g_sel = jnp.where(take, g, g_sel)
        g_best = jnp.where(take, gscore[g], g_best)

    def pick(rows, i):
        out = rows[i]
        for g in range(1, N_EXPERT_GROUPS):
            out = jnp.where(g_sel == g, rows[g * npg + i], out)
        return out

    xs = [pick(s, i) for i in range(npg)]
    i1 = jnp.zeros_like(g_sel)
    v1 = xs[0]
    for i in range(1, npg):
        take = xs[i] > v1
        i1 = jnp.where(take, i, i1)
        v1 = jnp.where(take, xs[i], v1)
    i2 = jnp.full_like(g_sel, -1)
    v2 = jnp.full_like(v1, -jnp.inf)
    for i in range(npg):
        take = (i1 != i) & ((i2 < 0) | (xs[i] > v2))
        i2 = jnp.where(take, i, i2)
        v2 = jnp.where(take, xs[i], v2)
    lo = jnp.minimum(i1, i2)
    hi = jnp.maximum(i1, i2)
    pair = jnp.zeros_like(lo)
    for pi_, (i, j) in enumerate(PAIRS):
        pair = jnp.where((lo == i) & (hi == j), pi_, pair)
    return g_sel * len(PAIRS) + pair


def _merge_kernel(x_ref, yg_ref, o_ref, gate_ref, mod_ref, g2_ref, wglu_ref, wat_ref, wo_ref,
                  rw_ref, rb_ref, xo_ref, h2_ref, route_ref):
    d = x_ref.shape[-1]
    tm = x_ref.shape[1]
    m = mod_ref[0]
    yg = jnp.concatenate([yg_ref[ci, 0] for ci in range(yg_ref.shape[0])], axis=0)
    ab = _dot(yg, wglu_ref[...])
    p_s = ab[:, 0:d] * jax.nn.sigmoid(ab[:, d:2 * d])
    p_a = _dot(o_ref[0], wat_ref[...])
    gate = gate_ref[0]
    mix = gate[:, 0:d].astype(F32) * p_s + gate[:, d:2 * d].astype(F32) * p_a
    x = x_ref[0] + m[2:3] * _dot(mix.astype(BF16), wo_ref[...])
    xo_ref[0] = x
    ms = jnp.mean(x * x, axis=-1, keepdims=True)
    y = x * lax.rsqrt(ms + NORM_EPS)
    h2 = (y * g2_ref[...]) * (1.0 + m[4:5]) + m[3:4]
    lanes = 128
    for kk in range(d // lanes):
        h2_ref[pl.ds(kk, tm, stride=d // lanes), :] = h2[:, kk * lanes:(kk + 1) * lanes]
    lt = _router_logits(h2, rw_ref).T[0:N_EXPERTS]
    cls = _route(lt, rb_ref[...])
    route_ref[...] = jnp.concatenate([cls.astype(F32), jnp.zeros((7,) + cls.shape[1:], F32)], axis=0)


def merge_and_route(xx, yg, o, gate, mods, g2, wglu, wat, wo, layer, rw_pad, rb, n_ctx_tiles):
    b, lt, d = xx.shape
    tm = TOKEN_TILE
    nj = lt // tm
    t = b * lt
    tok = lambda i, j: (i, j, 0)
    const = lambda i, j: (0, 0)

    def mod_idx(i, j):
        return (jnp.where(j < n_ctx_tiles, b, i), 0, 0)

    return pl.pallas_call(
        _merge_kernel,
        grid=(b, nj),
        in_specs=[
            pl.BlockSpec((1, tm, d), tok),
            pl.BlockSpec((tm // SSM_CHUNK, 1, SSM_CHUNK, yg.shape[-1]), lambda i, j: (j, i, 0, 0)),
            pl.BlockSpec((1, tm, o.shape[-1]), tok),
            pl.BlockSpec((1, tm, gate.shape[-1]), tok),
            pl.BlockSpec((1, 6, d), mod_idx),
            pl.BlockSpec((1, d), const),
            pl.BlockSpec((None,) + wglu.shape[1:], lambda i, j: (layer, 0, 0)),
            pl.BlockSpec((None,) + wat.shape[1:], lambda i, j: (layer, 0, 0)),
            pl.BlockSpec((None,) + wo.shape[1:], lambda i, j: (layer, 0, 0)),
            pl.BlockSpec(rw_pad.shape, const),
            pl.BlockSpec(rb.shape, const),
        ],
        out_specs=[
            pl.BlockSpec((1, tm, d), tok),
            pl.BlockSpec((tm * (d // 128), 128), lambda i, j: (i * nj + j, 0)),
            pl.BlockSpec((8, tm), lambda i, j: (0, i * nj + j)),
        ],
        out_shape=[
            jax.ShapeDtypeStruct((b, lt, d), F32),
            jax.ShapeDtypeStruct((t * (d // 128), 128), F32),
            jax.ShapeDtypeStruct((8, t), F32),
        ],
        input_output_aliases={0: 0},
        compiler_params=_cparams(("arbitrary", "arbitrary")),
        name="merge_and_route",
    )(xx, yg, o, gate, mods, g2.reshape(1, d), wglu, wat, wo, rw_pad, rb)


def _rank_kernel(route_ref, o_ref, cnt_ref):
    step = pl.program_id(0)
    n = route_ref.shape[1]

    @pl.when(step == 0)
    def _():
        cnt_ref[...] = jnp.zeros_like(cnt_ref)

    cls = route_ref[0:1, :]
    ncl = cnt_ref.shape[0]
    member = lax.broadcasted_iota(jnp.int32, (ncl, n), 0).astype(F32) == cls
    oh = member.astype(BF16)
    earlier = (lax.broadcasted_iota(jnp.int32, (n, n), 0) < lax.broadcasted_iota(jnp.int32, (n, n), 1)).astype(BF16)
    before = _dot(oh, earlier)
    run = cnt_ref[:, 0:1]
    rank = jnp.sum(jnp.where(member, before + run, 0.0), axis=0, keepdims=True)
    cnt_ref[...] = cnt_ref[...] + jnp.sum(member.astype(F32), axis=1, keepdims=True)
    o_ref[...] = jnp.concatenate([rank, cls, jnp.zeros((6, n), F32)], axis=0).astype(jnp.int32)


def class_ranks(route):
    t = route.shape[1]
    tm = TOKEN_TILE
    return pl.pallas_call(
        _rank_kernel,
        grid=(t // tm,),
        in_specs=[pl.BlockSpec((8, tm), lambda i: (0, i))],
        out_specs=[pl.BlockSpec((8, tm), lambda i: (0, i)), pl.BlockSpec((LANES, LANES), lambda i: (0, 0))],
        out_shape=[jax.ShapeDtypeStruct((8, t), jnp.int32), jax.ShapeDtypeStruct((LANES, LANES), F32)],
        compiler_params=_cparams(("arbitrary",)),
        name="class_ranks",
    )(route)


def block_tables(counts, t):
    r = EXPERT_BLOCK
    ncls = N_CLASSES
    cap = -(-(t + ncls * (r - 1)) // r) * r
    nblk = cap // r
    padded = (counts + r - 1) // r * r
    pend = jnp.cumsum(padded)
    pstart = pend - padded
    n_valid = pend[-1] // r
    blk = jnp.arange(nblk, dtype=jnp.int32)
    blk_cls = jnp.minimum(jnp.sum((pend[None, :] <= (blk * r)[:, None]).astype(jnp.int32), axis=1), ncls - 1)
    last_cls = jnp.max(jnp.where(counts > 0, jnp.arange(ncls, dtype=jnp.int32), 0))
    blk_cls = jnp.where(blk < n_valid, blk_cls, last_cls)
    pair_lo = jnp.array([p[0] for p in PAIRS], jnp.int32)
    pair_hi = jnp.array([p[1] for p in PAIRS], jnp.int32)
    grp = blk_cls // len(PAIRS)
    pr = blk_cls % len(PAIRS)
    e_lo = grp * EXPERTS_PER_GROUP + pair_lo[pr]
    e_hi = grp * EXPERTS_PER_GROUP + pair_hi[pr]
    pstart_pad = jnp.pad(pstart.astype(jnp.int32), (0, 32 - ncls))
    return pstart_pad, e_lo, e_hi, n_valid.astype(jnp.int32).reshape(1), cap


TOKEN_SUBLANES = 8
LANES = 128
GATHER_UNROLL = 8


def _slot_copies(pstart_ref, cls_ref, rank_ref, n, make_copy, sem):
    assert n % GATHER_UNROLL == 0

    def issue(i, carry):
        for uu in range(GATHER_UNROLL):
            r = i * GATHER_UNROLL + uu
            slot = pstart_ref[cls_ref[0, 0, r]] + rank_ref[0, 0, r]
            make_copy(r, slot, sem).start(priority=uu % 2)
        return carry
    lax.fori_loop(0, n // GATHER_UNROLL, issue, 0)


def _tile_rows(ref, first, n):
    ts = TOKEN_SUBLANES
    return ref.at[pl.ds(pl.multiple_of(first * ts, ts), n * ts)]


def _tiles_to_rows(ref, n):
    ts = TOKEN_SUBLANES
    return jnp.concatenate([ref[pl.ds(kk, n, stride=ts), :] for kk in range(ts)], axis=1)


def _dispatch_kernel(pstart_ref, cls_ref, rank_ref, src_ref, zero_ref, dst_ref, sem):
    del zero_ref
    n = cls_ref.shape[-1]

    def make_copy(r, slot, s):
        return pltpu.make_async_copy(_tile_rows(src_ref, r, 1), _tile_rows(dst_ref, slot, 1), s)

    _slot_copies(pstart_ref, cls_ref, rank_ref, n, make_copy, sem)
    pltpu.make_async_copy(src_ref, _tile_rows(dst_ref, 0, n), sem).wait()


def dispatch_tokens(h2, pstart, cls_blk, rank_blk, cap):
    ts = TOKEN_SUBLANES
    nblk, _, n = cls_blk.shape
    blk = lambda i, ps: (i, 0, 0)
    grid_spec = pltpu.PrefetchScalarGridSpec(
        num_scalar_prefetch=1,
        grid=(nblk,),
        in_specs=[
            pl.BlockSpec((1, 1, n), blk, memory_space=pltpu.SMEM),
            pl.BlockSpec((1, 1, n), blk, memory_space=pltpu.SMEM),
            pl.BlockSpec((n * ts, LANES), lambda i, ps: (i, 0)),
            pl.BlockSpec(memory_space=pl.ANY),
        ],
        out_specs=pl.BlockSpec(memory_space=pl.ANY),
        scratch_shapes=[pltpu.SemaphoreType.DMA(())],
    )
    return pl.pallas_call(
        _dispatch_kernel,
        grid_spec=grid_spec,
        out_shape=jax.ShapeDtypeStruct((cap * ts, LANES), h2.dtype),
        input_output_aliases={4: 0},
        compiler_params=_cparams(("arbitrary",)),
        name="dispatch_tokens",
    )(pstart, cls_blk, rank_blk, h2, jnp.zeros((cap * ts, LANES), h2.dtype))


def _combine_kernel(pstart_ref, cls_ref, rank_ref, tab_ref, x_ref, mod_ref, gf_ref, o_ref, buf_ref, sem, *, final):
    n = x_ref.shape[1]

    def make_copy(r, slot, s):
        return pltpu.make_async_copy(_tile_rows(tab_ref, slot, 1), _tile_rows(buf_ref, r, 1), s)

    _slot_copies(pstart_ref, cls_ref, rank_ref, n, make_copy, sem)
    pltpu.make_async_copy(_tile_rows(tab_ref, 0, n), buf_ref, sem).wait()
    g2 = mod_ref[0][5:6]
    x = x_ref[0] + g2 * _tiles_to_rows(buf_ref, n)
    if final:
        ms = jnp.mean(x * x, axis=-1, keepdims=True)
        x = (x * lax.rsqrt(ms + NORM_EPS)) * gf_ref[...]
    o_ref[0] = x


def combine_residual(xx, ys, pstart, cls_blk, rank_blk, mods, n_ctx_tiles, final_g=None):
    b, lt, d = xx.shape
    tm = TOKEN_TILE
    nj = lt // tm
    final = final_g is not None
    off = n_ctx_tiles if final else 0
    tok = lambda i, j, ps: (i, j + off, 0)
    blk = lambda i, j, ps: (i * nj + j + off, 0, 0)

    def mod_idx(i, j, ps):
        return (jnp.where(j + off < n_ctx_tiles, b, i), 0, 0)

    grid_spec = pltpu.PrefetchScalarGridSpec(
        num_scalar_prefetch=1,
        grid=(b, nj - off),
        in_specs=[
            pl.BlockSpec((1, 1, tm), blk, memory_space=pltpu.SMEM),
            pl.BlockSpec((1, 1, tm), blk, memory_space=pltpu.SMEM),
            pl.BlockSpec(memory_space=pl.ANY),
            pl.BlockSpec((1, tm, d), tok),
            pl.BlockSpec((1, 6, d), mod_idx),
            pl.BlockSpec((1, d), lambda i, j, ps: (0, 0)),
        ],
        out_specs=pl.BlockSpec((1, tm, d), lambda i, j, ps: (i, j, 0)),
        scratch_shapes=[pltpu.VMEM((tm * TOKEN_SUBLANES, LANES), F32), pltpu.SemaphoreType.DMA(())],
    )
    gf = (final_g if final else jnp.ones((d,), F32)).astype(F32).reshape(1, d)
    return pl.pallas_call(
        functools.partial(_combine_kernel, final=final),
        grid_spec=grid_spec,
        out_shape=jax.ShapeDtypeStruct((b, lt - off * tm, d), F32),
        input_output_aliases={} if final else {4: 0},
        compiler_params=_cparams(("arbitrary", "arbitrary")),
        name="combine_final" if final else "combine_residual",
    )(pstart, cls_blk, rank_blk, ys, xx, mods, gf)


def _router_logits(h2, rw_ref):
    n = h2.shape[0]
    hi = h2.astype(BF16)
    lo = (h2 - hi.astype(F32)).astype(BF16)
    pp = _dot(jnp.concatenate([hi, lo], axis=0), rw_ref[...])
    return (pp[0:n, 0:LANES] + pp[0:n, LANES:2 * LANES]) + (pp[n:2 * n, 0:LANES] + pp[n:2 * n, LANES:2 * LANES])


def _expert_kernel(elo_ref, ehi_ref, nv_ref, x_ref, rw_ref, w1a, w3a, w2a, w1b, w3b, w2b, o_ref):
    i = pl.program_id(0)
    n = x_ref.shape[0] // TOKEN_SUBLANES

    @pl.when(i < nv_ref[0])
    def _():
        x = _tiles_to_rows(x_ref, n)
        xb = x.astype(BF16)

        def ffn(w1, w3, w2):
            h1 = _dot(xb, w1[...])
            h3 = _dot(xb, w3[...])
            hid = (h1 * jax.nn.sigmoid(h1)) * h3
            return _dot(hid.astype(BF16), w2[...])

        aff = jax.nn.sigmoid(_router_logits(x, rw_ref))
        lane = lax.broadcasted_iota(jnp.int32, aff.shape, 1)
        a_lo = jnp.sum(jnp.where(lane == elo_ref[i], aff, 0.0), axis=1, keepdims=True)
        a_hi = jnp.sum(jnp.where(lane == ehi_ref[i], aff, 0.0), axis=1, keepdims=True)
        tot = a_lo + a_hi
        y = (a_lo / tot) * ffn(w1a, w3a, w2a) + (a_hi / tot) * ffn(w1b, w3b, w2b)
        for kk in range(TOKEN_SUBLANES):
            o_ref[pl.ds(kk, n, stride=TOKEN_SUBLANES), :] = y[:, kk * LANES:(kk + 1) * LANES]

    @pl.when(i >= nv_ref[0])
    def _():
        o_ref[...] = jnp.zeros_like(o_ref)


def expert_blocks(xs, rw_pad, e_lo, e_hi, n_valid, w1, w3, w2, layer):
    ts = TOKEN_SUBLANES
    cap = xs.shape[0] // ts
    d = ts * LANES
    r = EXPERT_BLOCK
    nblk = cap // r
    de = w1.shape[-1]
    row = lambda i, elo, ehi, nv: (i, 0)
    wlo = lambda i, elo, ehi, nv: (layer, elo[i], 0, 0)
    whi = lambda i, elo, ehi, nv: (layer, ehi[i], 0, 0)
    grid_spec = pltpu.PrefetchScalarGridSpec(
        num_scalar_prefetch=3,
        grid=(nblk,),
        in_specs=[
            pl.BlockSpec((r * ts, LANES), row),
            pl.BlockSpec(rw_pad.shape, lambda i, elo, ehi, nv: (0, 0)),
            pl.BlockSpec((None, None, d, de), wlo),
            pl.BlockSpec((None, None, d, de), wlo),
            pl.BlockSpec((None, None, de, d), wlo),
            pl.BlockSpec((None, None, d, de), whi),
            pl.BlockSpec((None, None, d, de), whi),
            pl.BlockSpec((None, None, de, d), whi),
        ],
        out_specs=pl.BlockSpec((r * ts, LANES), row),
    )
    return pl.pallas_call(
        _expert_kernel,
        grid_spec=grid_spec,
        out_shape=jax.ShapeDtypeStruct((cap * ts, LANES), F32),
        compiler_params=_cparams(("arbitrary",)),
        name="expert_blocks",
    )(e_lo, e_hi, n_valid, xs, rw_pad, w1, w3, w2, w1, w3, w2)


def kernel(x, c, ctx, c_ctx, ada_w, ada_b, norm1_g, norm2_g, final_g, w_in,
           ssm_lam_re, ssm_lam_im, ssm_log_dt, ssm_b_re, ssm_b_im, ssm_c_re, ssm_c_im, ssm_d, w_glu,
           attn_lambda, attn_subln_g, w_attn_out, w_o, router_w, router_b, exp_w1, exp_w3, exp_w2):
    b, s, d = x.shape
    lc = ctx.shape[1]
    depth = w_in.shape[0]
    tm = TOKEN_TILE
    assert lc % tm == 0 and s % tm == 0 and s % GRID_W == 0 and lc % SSM_CHUNK == 0
    n_ctx_tiles = lc // tm
    lt = lc + s

    xx = jnp.concatenate([ctx, x], axis=1)

    n_mod_rows = -(-(b + 1) // 8) * 8
    c_all = jnp.zeros((n_mod_rows, d), F32).at[:b].set(c).at[b].set(c_ctx)
    mods_all = ada_modulation(c_all, ada_w, ada_b).reshape(depth, n_mod_rows, 6, d)

    cos_t, sa_t, sb_t = rope_tables(lc, s)
    assert d == TOKEN_SUBLANES * LANES
    rw_f = jnp.zeros((d, LANES), F32).at[:, :N_EXPERTS].set(router_w.astype(F32))
    rw_hi = rw_f.astype(BF16)
    rw_lo = (rw_f - rw_hi.astype(F32)).astype(BF16)
    rw_pad = jnp.concatenate([rw_hi, rw_lo], axis=1)
    rb = router_b.astype(F32).reshape(N_EXPERTS, 1)

    w_in_bf = w_in.astype(BF16)
    w_glu_bf = w_glu.astype(BF16)
    w_at_bf = w_attn_out.astype(BF16)
    w_o_bf = w_o.astype(BF16)
    e1_bf = exp_w1.astype(BF16)
    e3_bf = exp_w3.astype(BF16)
    e2_bf = exp_w2.astype(BF16)

    w1s, w2s, a16 = jax.vmap(s5_chunk_weights)(ssm_lam_re, ssm_lam_im, ssm_log_dt, ssm_b_re, ssm_b_im,
                                               ssm_c_re, ssm_c_im, ssm_d)

    for l in range(depth):
        lam_init = 0.8 - 0.6 * math.exp(-0.3 * l)
        mods = mods_all[l]
        u, q, k, v, gate = in_projection(xx, mods, norm1_g[l], w_in_bf, l,
                                         cos_t, sa_t, sb_t, n_ctx_tiles)

        yg = s5_mixer(u, w1s, w2s, a16, l, lc)

        lf = attn_lambda[l].astype(F32)
        lam = jnp.exp(jnp.sum(lf[0] * lf[1])) - jnp.exp(jnp.sum(lf[2] * lf[3])) + lam_init
        o = diff_attention(q, k, v, lam, attn_subln_g[l].astype(F32), lc, lam_init)

        xx, h2, route = merge_and_route(xx, yg, o, gate, mods, norm2_g[l], w_glu_bf, w_at_bf, w_o_bf, l,
                                        rw_pad, rb, n_ctx_tiles)

        ranks, hist = class_ranks(route)
        n_tiles = b * lt // tm
        rank_blk = ranks[0].reshape(n_tiles, 1, tm)
        cls_blk = ranks[1].reshape(n_tiles, 1, tm)
        counts = hist[:N_CLASSES, 0].astype(jnp.int32)
        pstart, e_lo, e_hi, n_valid, cap = block_tables(counts, b * lt)
        xs = dispatch_tokens(h2, pstart, cls_blk, rank_blk, cap)
        ys = expert_blocks(xs, rw_pad, e_lo, e_hi, n_valid, e1_bf, e3_bf, e2_bf, l)
        last = l == depth - 1
        xx = combine_residual(xx, ys, pstart, cls_blk, rank_blk, mods, n_ctx_tiles,
                              final_g=final_g if last else None)

    return xx
```

```python
import functools
import math

import jax
import jax.numpy as jnp
from jax import lax
from jax.experimental import pallas as pl
from jax.experimental.pallas import tpu as pltpu

F32 = jnp.float32
BF16 = jnp.bfloat16
HIGHEST = lax.Precision.HIGHEST

GRID_W = 64
SSM_GROUP = 16
SSM_STATE = 64
N_HEADS = 4
QK_HEAD_DIM = 64
V_HEAD_DIM = 128
ROPE_BASE = 10000.0
ROPE_AXIS_FREQS = QK_HEAD_DIM // 4
N_EXPERTS = 16
EXPERTS_PER_GROUP = 4
N_EXPERT_GROUPS = N_EXPERTS // EXPERTS_PER_GROUP
NORM_EPS = 1e-6
SUBLN_EPS = 1e-5

PAIRS = ((0, 1), (0, 2), (0, 3), (1, 2), (1, 3), (2, 3))
N_CLASSES = N_EXPERT_GROUPS * len(PAIRS)

TOKEN_TILE = 256
SSM_CHUNK = 16
EXPERT_BLOCK = 256
VMEM_LIMIT = 56 * 1024 * 1024


def _dot(a, b):
    return jnp.dot(a, b, preferred_element_type=F32)


def _cparams(sem):
    return pltpu.CompilerParams(dimension_semantics=sem, vmem_limit_bytes=VMEM_LIMIT)


def _ada_kernel(c_ref, w_ref, b_ref, o_ref):
    c = c_ref[...]
    s = c * jax.nn.sigmoid(c)
    o_ref[0] = jnp.dot(s, w_ref[0], precision=HIGHEST, preferred_element_type=F32) + b_ref[0]


def ada_modulation(c_all, ada_w, ada_b):
    depth, d, n = ada_w.shape
    r = c_all.shape[0]
    tn = n // 4
    return pl.pallas_call(
        _ada_kernel,
        grid=(depth, n // tn),
        in_specs=[
            pl.BlockSpec((r, d), lambda l, j: (0, 0)),
            pl.BlockSpec((1, d, tn), lambda l, j: (l, 0, j)),
            pl.BlockSpec((1, 1, tn), lambda l, j: (l, 0, j)),
        ],
        out_specs=pl.BlockSpec((1, r, tn), lambda l, j: (l, 0, j)),
        out_shape=jax.ShapeDtypeStruct((depth, r, n), F32),
        compiler_params=_cparams(("arbitrary", "arbitrary")),
        name="ada_modulation",
    )(c_all, ada_w, ada_b.reshape(depth, 1, n))


def _inproj_kernel(x_ref, mod_ref, g_ref, w_ref, cos_ref, sa_ref, sb_ref,
                   u_ref, q_ref, k_ref, v_ref, gate_ref, *, d_ssm, d_qk, d_v):
    x = x_ref[0]
    m = mod_ref[0]
    ms = jnp.mean(x * x, axis=-1, keepdims=True)
    y = x * lax.rsqrt(ms + NORM_EPS)
    h = (y * g_ref[...]) * (1.0 + m[1:2]) + m[0:1]
    hb = h.astype(BF16)

    cos = cos_ref[...]
    sa = sa_ref[...]
    sb = sb_ref[...]
    half = ROPE_AXIS_FREQS

    def rope(t):
        return t * cos + pltpu.roll(t, d_qk - half, 1) * sa + pltpu.roll(t, half, 1) * sb

    o0 = 0
    u = _dot(hb, w_ref[:, o0:o0 + d_ssm]).astype(BF16)
    c = SSM_CHUNK
    for ci in range(u.shape[0] // c):
        u_ref[ci, 0] = u[ci * c:(ci + 1) * c, :]
    o0 += d_ssm
    q_ref[0] = (rope(_dot(hb, w_ref[:, o0:o0 + d_qk])) * (QK_HEAD_DIM ** -0.5 * math.log2(math.e))).astype(BF16)
    o0 += d_qk
    k_ref[0] = rope(_dot(hb, w_ref[:, o0:o0 + d_qk])).astype(BF16)
    o0 += d_qk
    v_ref[0] = _dot(hb, w_ref[:, o0:o0 + d_v]).astype(BF16)
    o0 += d_v
    n_gate = gate_ref.shape[-1]
    gate_ref[0] = jax.nn.sigmoid(_dot(hb, w_ref[:, o0:o0 + n_gate])).astype(BF16)


def in_projection(xx, mods, g, w_all, layer, cos_t, sa_t, sb_t, n_ctx_tiles):
    b, lt, d = xx.shape
    nj = lt // TOKEN_TILE
    n_lat = b
    d_ssm = d // 2
    d_qk = N_HEADS * 2 * QK_HEAD_DIM
    d_v = N_HEADS * V_HEAD_DIM
    n_gate = 2 * d
    tm = TOKEN_TILE

    def mod_idx(i, j):
        return (jnp.where(j < n_ctx_tiles, n_lat, i), 0, 0)

    tok = lambda i, j: (i, j, 0)
    kern = functools.partial(_inproj_kernel, d_ssm=d_ssm, d_qk=d_qk, d_v=d_v)
    return pl.pallas_call(
        kern,
        grid=(b, nj),
        in_specs=[
            pl.BlockSpec((1, tm, d), tok),
            pl.BlockSpec((1, 6, d), mod_idx),
            pl.BlockSpec((1, d), lambda i, j: (0, 0)),
            pl.BlockSpec((None,) + w_all.shape[1:], lambda i, j: (layer, 0, 0)),
            pl.BlockSpec((tm, d_qk), lambda i, j: (j, 0)),
            pl.BlockSpec((tm, d_qk), lambda i, j: (j, 0)),
            pl.BlockSpec((tm, d_qk), lambda i, j: (j, 0)),
        ],
        out_specs=[
            pl.BlockSpec((tm // SSM_CHUNK, 1, SSM_CHUNK, d_ssm), lambda i, j: (j, i, 0, 0)),
            pl.BlockSpec((1, tm, d_qk), tok),
            pl.BlockSpec((1, tm, d_qk), tok),
            pl.BlockSpec((1, tm, d_v), tok),
            pl.BlockSpec((1, tm, n_gate), tok),
        ],
        out_shape=[
            jax.ShapeDtypeStruct((lt // SSM_CHUNK, b, SSM_CHUNK, d_ssm), BF16),
            jax.ShapeDtypeStruct((b, lt, d_qk), BF16),
            jax.ShapeDtypeStruct((b, lt, d_qk), BF16),
            jax.ShapeDtypeStruct((b, lt, d_v), BF16),
            jax.ShapeDtypeStruct((b, lt, n_gate), BF16),
        ],
        compiler_params=_cparams(("arbitrary", "arbitrary")),
        name="in_projection",
    )(xx, mods, g.reshape(1, d), w_all, cos_t, sa_t, sb_t)


def rope_tables(lc, s):
    rows = s // GRID_W
    row = jnp.repeat(jnp.arange(rows), GRID_W).astype(F32)
    col = jnp.tile(jnp.arange(GRID_W), rows).astype(F32)
    inv = ROPE_BASE ** (-jnp.arange(ROPE_AXIS_FREQS, dtype=F32) / ROPE_AXIS_FREQS)
    ang_r = row[:, None] * inv
    ang_c = col[:, None] * inv
    ang = jnp.concatenate([ang_r, ang_r, ang_c, ang_c], axis=-1)
    cos = jnp.concatenate([jnp.ones((lc, QK_HEAD_DIM), F32), jnp.cos(ang)], axis=0)
    sin = jnp.concatenate([jnp.zeros((lc, QK_HEAD_DIM), F32), jnp.sin(ang)], axis=0)
    first_half = (jnp.arange(QK_HEAD_DIM) % (2 * ROPE_AXIS_FREQS)) < ROPE_AXIS_FREQS
    sa = jnp.where(first_half, -sin, 0.0)
    sb = jnp.where(first_half, 0.0, sin)
    reps = N_HEADS * 2
    return jnp.tile(cos, (1, reps)), jnp.tile(sa, (1, reps)), jnp.tile(sb, (1, reps))


def _cmul(ar, ai, br, bi):
    return ar * br - ai * bi, ar * bi + ai * br


def s5_chunk_weights(lam_re, lam_im, log_dt, b_re, b_im, c_re, c_im, d_skip):
    c = SSM_CHUNK
    lr = lam_re.astype(F32)
    li = lam_im.astype(F32)
    dt = jnp.exp(log_dt.astype(F32))[..., None]
    decay = jnp.exp(lr * dt)
    ab_re = decay * jnp.cos(li * dt)
    ab_im = decay * jnp.sin(li * dt)
    den = lr * lr + li * li
    nr = ab_re - 1.0
    ni = ab_im
    cr = ((nr * lr + ni * li) / den)[..., None]
    ci = ((ni * lr - nr * li) / den)[..., None]
    br = b_re.astype(F32)
    bi = b_im.astype(F32)
    bb_re = cr * br - ci * bi
    bb_im = cr * bi + ci * br
    ccr = c_re.astype(F32)
    cci = c_im.astype(F32)

    pr = [jnp.ones_like(ab_re)]
    pi = [jnp.zeros_like(ab_im)]
    for _ in range(c):
        r, i = _cmul(pr[-1], pi[-1], ab_re, ab_im)
        pr.append(r)
        pi.append(i)
    pw_re = jnp.stack(pr)
    pw_im = jnp.stack(pi)

    pb_re, pb_im = _cmul(pw_re[..., None], pw_im[..., None], bb_re[None], bb_im[None])
    cp_re, cp_im = _cmul(ccr[None], cci[None], pw_re[:, :, :, None, :], pw_im[:, :, :, None, :])
    kk = (jnp.einsum('dgop,tdgpi->tdgoi', ccr, pb_re[:c], precision=HIGHEST)
          - jnp.einsum('dgop,tdgpi->tdgoi', cci, pb_im[:c], precision=HIGHEST))

    g = lr.shape[1]
    h = SSM_GROUP
    p = SSM_STATE
    ii = jnp.arange(c)[:, None]
    jj = jnp.arange(c)[None, :]
    lags = jnp.arange(c)[:, None, None]
    shift_f = ((jj - ii)[None] == lags).astype(F32)
    shift_b = ((ii - jj)[None] == lags).astype(F32)
    tt = (jnp.einsum('tij,tgon->ginjo', shift_f, kk[:, 0], precision=HIGHEST)
          + jnp.einsum('tij,tgon->ginjo', shift_b, kk[:, 1], precision=HIGHEST))
    dd = d_skip.astype(F32).reshape(g, h)
    eye_t = jnp.eye(c, dtype=F32)
    eye_h = jnp.eye(h, dtype=F32)
    tt = tt + (eye_t[None, :, None, :, None] * eye_h[None, None, :, None, :]
               * dd[:, None, :, None, None])
    tt = tt.reshape(g, c * h, c * h)

    def s_cols(pbx, d, rev):
        m = pbx[:c, d]
        if rev:
            m = m[::-1]
        return jnp.transpose(m, (1, 0, 3, 2)).reshape(g, c * h, p)

    w1 = jnp.concatenate([
        tt,
        s_cols(pb_re, 0, True), s_cols(pb_re, 1, False),
        s_cols(pb_im, 0, True), s_cols(pb_im, 1, False)], axis=-1)

    def o_rows(cpx, d, sign, fwd):
        m = cpx[1:c + 1, d]
        if not fwd:
            m = m[::-1]
        return sign * jnp.transpose(m, (1, 3, 0, 2)).reshape(g, p, c * h)

    w2 = jnp.concatenate([
        o_rows(cp_re, 0, 1.0, True), o_rows(cp_re, 1, 1.0, False),
        o_rows(cp_im, 0, -1.0, True), o_rows(cp_im, 1, -1.0, False)], axis=1)

    a16 = jnp.zeros((g, 8, 2 * p), F32)
    a16 = a16.at[:, 0, :].set(jnp.concatenate([pw_re[c, 0], pw_re[c, 1]], axis=-1))
    a16 = a16.at[:, 1, :].set(jnp.concatenate([pw_im[c, 0], pw_im[c, 1]], axis=-1))
    return w1.astype(BF16), w2.astype(BF16), a16


def _s5_kernel(x_ref, w1_ref, w2_ref, a_ref, y_ref, z_ref, hin_ref, *, nb, nc_ctx, nc_lat):
    ch = x_ref.shape[-1]
    p = SSM_STATE
    z_ref[...] = _dot(x_ref[0], w1_ref[0])
    a_re = a_ref[0, 0:1, :]
    a_im = a_ref[0, 1:2, :]
    fwd_lane = lax.broadcasted_iota(jnp.int32, (nb, 2 * p), 1) < p

    def phase(base, n, carry):
        def step(i, hc):
            h_re, h_im = hc
            rf = pl.multiple_of((base + i) * nb, nb)
            rb = pl.multiple_of((base + n - 1 - i) * nb, nb)
            hin_ref[pl.ds(rf, nb), 0:p] = h_re[:, 0:p]
            hin_ref[pl.ds(rb, nb), p:2 * p] = h_re[:, p:2 * p]
            hin_ref[pl.ds(rf, nb), 2 * p:3 * p] = h_im[:, 0:p]
            hin_ref[pl.ds(rb, nb), 3 * p:4 * p] = h_im[:, p:2 * p]
            s_re = jnp.where(fwd_lane, z_ref[pl.ds(rf, nb), ch:ch + 2 * p],
                             z_ref[pl.ds(rb, nb), ch:ch + 2 * p])
            s_im = jnp.where(fwd_lane, z_ref[pl.ds(rf, nb), ch + 2 * p:ch + 4 * p],
                             z_ref[pl.ds(rb, nb), ch + 2 * p:ch + 4 * p])
            n_re = a_re * h_re - a_im * h_im + s_re
            n_im = a_re * h_im + a_im * h_re + s_im
            return n_re, n_im
        return lax.fori_loop(0, n, step, carry)

    zero = jnp.zeros((nb, 2 * p), F32)
    hc = phase(0, nc_ctx, (zero, zero))
    phase(nc_ctx, nc_lat, hc)
    y = z_ref[:, 0:ch] + _dot(hin_ref[...].astype(BF16), w2_ref[0])
    y_ref[0] = jax.nn.gelu(y).astype(BF16)


def _regroup_kernel(x_ref, o_ref, *, n_groups):
    h = SSM_GROUP
    for t in range(SSM_CHUNK):
        for g in range(n_groups):
            src = (t * n_groups + g) * h
            o_ref[g, :, t * h:(t + 1) * h] = x_ref[:, src:src + h]


def _ungroup_kernel(y_ref, o_ref, *, n_groups):
    h = SSM_GROUP
    for t in range(SSM_CHUNK):
        for g in range(n_groups):
            dst = (t * n_groups + g) * h
            o_ref[:, dst:dst + h] = y_ref[g, :, t * h:(t + 1) * h]


REGROUP_ROWS = 128


def _regroup(x2, n_groups, inverse):
    c = SSM_CHUNK
    h = SSM_GROUP
    if inverse:
        g, m, _ = x2.shape
    else:
        m = x2.shape[0]
        g = n_groups
    rb = min(REGROUP_ROWS, m)
    assert m % rb == 0
    wide = pl.BlockSpec((rb, c * g * h), lambda i: (i, 0))
    grouped = pl.BlockSpec((g, rb, c * h), lambda i: (0, i, 0))
    kern = functools.partial(_ungroup_kernel if inverse else _regroup_kernel, n_groups=g)
    return pl.pallas_call(
        kern,
        grid=(m // rb,),
        in_specs=[grouped if inverse else wide],
        out_specs=wide if inverse else grouped,
        out_shape=jax.ShapeDtypeStruct((m, c * g * h) if inverse else (g, m, c * h), x2.dtype),
        compiler_params=_cparams(("arbitrary",)),
        name="s5_ungroup" if inverse else "s5_regroup",
    )(x2)


def s5_mixer(u4, w1, w2, a16, layer, lc):
    nc, b, c, dssm = u4.shape
    h = SSM_GROUP
    g = dssm // h
    lt = nc * c
    m = nc * b
    ug = _regroup(u4.reshape(m, c * dssm), g, False)
    kern = functools.partial(_s5_kernel, nb=b, nc_ctx=lc // c, nc_lat=(lt - lc) // c)
    yg = pl.pallas_call(
        kern,
        grid=(g,),
        in_specs=[
            pl.BlockSpec((1, m, c * h), lambda i: (i, 0, 0)),
            pl.BlockSpec((None, 1) + w1.shape[2:], lambda i: (layer, i, 0, 0)),
            pl.BlockSpec((None, 1) + w2.shape[2:], lambda i: (layer, i, 0, 0)),
            pl.BlockSpec((None, 1, 8, 2 * SSM_STATE), lambda i: (layer, i, 0, 0)),
        ],
        out_specs=pl.BlockSpec((1, m, c * h), lambda i: (i, 0, 0)),
        out_shape=jax.ShapeDtypeStruct((g, m, c * h), BF16),
        scratch_shapes=[pltpu.VMEM((m, w1.shape[-1]), F32), pltpu.VMEM((m, 4 * SSM_STATE), F32)],
        compiler_params=_cparams(("arbitrary",)),
        name="s5_mixer",
    )(ug, w1, w2, a16)
    return _regroup(yg, g, True).reshape(nc, b, c, dssm)


def _attn_kernel(lam_ref, q_ref, k_ref, v_ref, g_ref, o_ref, *, n_ctx_tiles, lc, out_scale):
    j = pl.program_id(2)
    lam = lam_ref[0]
    q = q_ref[0]
    lo = lax.broadcasted_iota(jnp.int32, q.shape, 1) < QK_HEAD_DIM
    zq = jnp.zeros_like(q)
    q1 = jnp.where(lo, q, zq)
    q2 = jnp.where(lo, zq, q)
    nt = (((1,), (1,)), ((), ()))

    def run(nk):
        k = k_ref[0, 0:nk, :]
        v = v_ref[0, 0:nk, :]
        s1 = lax.dot_general(q1, k, nt, preferred_element_type=F32)
        s2 = lax.dot_general(q2, k, nt, preferred_element_type=F32)
        e1 = jnp.exp2(s1 - jnp.max(s1, axis=-1, keepdims=True))
        e2 = jnp.exp2(s2 - jnp.max(s2, axis=-1, keepdims=True))
        l1 = jnp.sum(e1, axis=-1, keepdims=True)
        l2 = jnp.sum(e2, axis=-1, keepdims=True)
        a = e1 - e2 * (lam * l1 / l2)
        o = _dot(a.astype(BF16), v) * (1.0 / l1)
        ms = jnp.mean(o * o, axis=-1, keepdims=True)
        o = (o * lax.rsqrt(ms + SUBLN_EPS)) * g_ref[...] * out_scale
        o_ref[0] = o.astype(BF16)

    @pl.when(j < n_ctx_tiles)
    def _():
        run(lc)

    @pl.when(j >= n_ctx_tiles)
    def _():
        run(k_ref.shape[1])


def diff_attention(q, k, v, lam, subln_g, lc, lam_init):
    b, lt, _ = q.shape
    tm = TOKEN_TILE
    nj = lt // tm
    hd = 2 * QK_HEAD_DIM
    kern = functools.partial(_attn_kernel, n_ctx_tiles=lc // tm, lc=lc, out_scale=1.0 - lam_init)
    return pl.pallas_call(
        kern,
        grid=(b, N_HEADS, nj),
        in_specs=[
            pl.BlockSpec(memory_space=pltpu.SMEM),
            pl.BlockSpec((1, tm, hd), lambda i, h, j: (i, j, h)),
            pl.BlockSpec((1, lt, hd), lambda i, h, j: (i, 0, h)),
            pl.BlockSpec((1, lt, V_HEAD_DIM), lambda i, h, j: (i, 0, h)),
            pl.BlockSpec((1, V_HEAD_DIM), lambda i, h, j: (0, 0)),
        ],
        out_specs=pl.BlockSpec((1, tm, V_HEAD_DIM), lambda i, h, j: (i, j, h)),
        out_shape=jax.ShapeDtypeStruct((b, lt, N_HEADS * V_HEAD_DIM), BF16),
        compiler_params=_cparams(("arbitrary", "arbitrary", "arbitrary")),
        name="diff_attention",
    )(lam.reshape(1), q, k, v, subln_g.reshape(1, V_HEAD_DIM))


def _route(logits_t, rb):
    aff = jax.nn.sigmoid(logits_t)
    sel = aff + rb
    s = [sel[e:e + 1] for e in range(N_EXPERTS)]
    npg = EXPERTS_PER_GROUP
    gscore = []
    for g in range(N_EXPERT_GROUPS):
        best = None
        for (i, j) in PAIRS:
            ps = s[g * npg + i] + s[g * npg + j]
            best = ps if best is None else jnp.maximum(best, ps)
        gscore.append(best)
    g_sel = jnp.zeros_like(gscore[0], dtype=jnp.int32)
    g_best = gscore[0]
    for g in range(1, N_EXPERT_GROUPS):
        take = gscore[g] > g_best
        g_sel = jnp.where(take, g, g_sel)
        g_best = jnp.where(take, gscore[g], g_best)

    def pick(rows, i):
        out = rows[i]
        for g in range(1, N_EXPERT_GROUPS):
            out = jnp.where(g_sel == g, rows[g * npg + i], out)
        return out

    xs = [pick(s, i) for i in range(npg)]
    i1 = jnp.zeros_like(g_sel)
    v1 = xs[0]
    for i in range(1, npg):
        take = xs[i] > v1
        i1 = jnp.where(take, i, i1)
        v1 = jnp.where(take, xs[i], v1)
    i2 = jnp.full_like(g_sel, -1)
    v2 = jnp.full_like(v1, -jnp.inf)
    for i in range(npg):
        take = (i1 != i) & ((i2 < 0) | (xs[i] > v2))
        i2 = jnp.where(take, i, i2)
        v2 = jnp.where(take, xs[i], v2)
    lo = jnp.minimum(i1, i2)
    hi = jnp.maximum(i1, i2)
    pair = jnp.zeros_like(lo)
    for pi_, (i, j) in enumerate(PAIRS):
        pair = jnp.where((lo == i) & (hi == j), pi_, pair)
    return g_sel * len(PAIRS) + pair


def _merge_kernel(x_ref, yg_ref, o_ref, gate_ref, mod_ref, g2_ref, wglu_ref, wat_ref, wo_ref,
                  rw_ref, rb_ref, xo_ref, h2_ref, route_ref, cnt_ref):
    d = x_ref.shape[-1]
    tm = x_ref.shape[1]
    m = mod_ref[0]
    yg = jnp.concatenate([yg_ref[ci, 0] for ci in range(yg_ref.shape[0])], axis=0)
    ab = _dot(yg, wglu_ref[...])
    p_s = ab[:, 0:d] * jax.nn.sigmoid(ab[:, d:2 * d])
    p_a = _dot(o_ref[0], wat_ref[...])
    gate = gate_ref[0]
    mix = gate[:, 0:d].astype(F32) * p_s + gate[:, d:2 * d].astype(F32) * p_a
    x = x_ref[0] + m[2:3] * _dot(mix.astype(BF16), wo_ref[...])
    xo_ref[0] = x
    ms = jnp.mean(x * x, axis=-1, keepdims=True)
    y = x * lax.rsqrt(ms + NORM_EPS)
    h2 = (y * g2_ref[...]) * (1.0 + m[4:5]) + m[3:4]
    lanes = 128
    for kk in range(d // lanes):
        h2_ref[pl.ds(kk, tm, stride=d // lanes), :] = h2[:, kk * lanes:(kk + 1) * lanes]
    lt = _router_logits(h2, rw_ref).T[0:N_EXPERTS]
    cls = _route(lt, rb_ref[...])

    @pl.when((pl.program_id(0) == 0) & (pl.program_id(1) == 0))
    def _():
        cnt_ref[...] = jnp.zeros_like(cnt_ref)

    member = lax.broadcasted_iota(jnp.int32, (cnt_ref.shape[0], tm), 0) == cls
    earlier = (lax.broadcasted_iota(jnp.int32, (tm, tm), 0) < lax.broadcasted_iota(jnp.int32, (tm, tm), 1))
    before = _dot(member.astype(BF16), earlier.astype(BF16))
    rank = jnp.sum(jnp.where(member, before + cnt_ref[:, 0:1], 0.0), axis=0, keepdims=True)
    cnt_ref[...] = cnt_ref[...] + jnp.sum(member.astype(F32), axis=1, keepdims=True)
    route_ref[...] = jnp.concatenate([rank.astype(jnp.int32), cls, jnp.zeros((6, tm), jnp.int32)], axis=0)


def merge_and_route(xx, yg, o, gate, mods, g2, wglu, wat, wo, layer, rw_pad, rb, n_ctx_tiles):
    b, lt, d = xx.shape
    tm = TOKEN_TILE
    nj = lt // tm
    t = b * lt
    tok = lambda i, j: (i, j, 0)
    const = lambda i, j: (0, 0)

    def mod_idx(i, j):
        return (jnp.where(j < n_ctx_tiles, b, i), 0, 0)

    return pl.pallas_call(
        _merge_kernel,
        grid=(b, nj),
        in_specs=[
            pl.BlockSpec((1, tm, d), tok),
            pl.BlockSpec((tm // SSM_CHUNK, 1, SSM_CHUNK, yg.shape[-1]), lambda i, j: (j, i, 0, 0)),
            pl.BlockSpec((1, tm, o.shape[-1]), tok),
            pl.BlockSpec((1, tm, gate.shape[-1]), tok),
            pl.BlockSpec((1, 6, d), mod_idx),
            pl.BlockSpec((1, d), const),
            pl.BlockSpec((None,) + wglu.shape[1:], lambda i, j: (layer, 0, 0)),
            pl.BlockSpec((None,) + wat.shape[1:], lambda i, j: (layer, 0, 0)),
            pl.BlockSpec((None,) + wo.shape[1:], lambda i, j: (layer, 0, 0)),
            pl.BlockSpec(rw_pad.shape, const),
            pl.BlockSpec(rb.shape, const),
        ],
        out_specs=[
            pl.BlockSpec((1, tm, d), tok),
            pl.BlockSpec((tm * (d // 128), 128), lambda i, j: (i * nj + j, 0)),
            pl.BlockSpec((8, tm), lambda i, j: (0, i * nj + j)),
            pl.BlockSpec((LANES, LANES), const),
        ],
        out_shape=[
            jax.ShapeDtypeStruct((b, lt, d), F32),
            jax.ShapeDtypeStruct((t * (d // 128), 128), F32),
            jax.ShapeDtypeStruct((8, t), jnp.int32),
            jax.ShapeDtypeStruct((LANES, LANES), F32),
        ],
        input_output_aliases={0: 0},
        compiler_params=_cparams(("arbitrary", "arbitrary")),
        name="merge_and_route",
    )(xx, yg, o, gate, mods, g2.reshape(1, d), wglu, wat, wo, rw_pad, rb)


def block_tables(counts, t):
    r = EXPERT_BLOCK
    ncls = N_CLASSES
    cap = -(-(t + ncls * (r - 1)) // r) * r
    nblk = cap // r
    padded = (counts + r - 1) // r * r
    pend = jnp.cumsum(padded)
    pstart = pend - padded
    n_valid = pend[-1] // r
    blk = jnp.arange(nblk, dtype=jnp.int32)
    blk_cls = jnp.minimum(jnp.sum((pend[None, :] <= (blk * r)[:, None]).astype(jnp.int32), axis=1), ncls - 1)
    last_cls = jnp.max(jnp.where(counts > 0, jnp.arange(ncls, dtype=jnp.int32), 0))
    blk_cls = jnp.where(blk < n_valid, blk_cls, last_cls)
    pair_lo = jnp.array([p[0] for p in PAIRS], jnp.int32)
    pair_hi = jnp.array([p[1] for p in PAIRS], jnp.int32)
    grp = blk_cls // len(PAIRS)
    pr = blk_cls % len(PAIRS)
    e_lo = grp * EXPERTS_PER_GROUP + pair_lo[pr]
    e_hi = grp * EXPERTS_PER_GROUP + pair_hi[pr]
    pstart_pad = jnp.pad(pstart.astype(jnp.int32), (0, 32 - ncls))
    return pstart_pad, e_lo, e_hi, n_valid.astype(jnp.int32).reshape(1), cap


TOKEN_SUBLANES = 8
LANES = 128
GATHER_UNROLL = 8


def _slot_copies(pstart_ref, cls_ref, rank_ref, n, make_copy, sem):
    assert n % GATHER_UNROLL == 0

    def issue(i, carry):
        for uu in range(GATHER_UNROLL):
            r = i * GATHER_UNROLL + uu
            slot = pstart_ref[cls_ref[0, 0, r]] + rank_ref[0, 0, r]
            make_copy(r, slot, sem).start(priority=uu % 2)
        return carry
    lax.fori_loop(0, n // GATHER_UNROLL, issue, 0)


def _tile_rows(ref, first, n):
    ts = TOKEN_SUBLANES
    return ref.at[pl.ds(pl.multiple_of(first * ts, ts), n * ts)]


def _tiles_to_rows(ref, n):
    ts = TOKEN_SUBLANES
    return jnp.concatenate([ref[pl.ds(kk, n, stride=ts), :] for kk in range(ts)], axis=1)


def _dispatch_kernel(pstart_ref, cls_ref, rank_ref, src_ref, zero_ref, dst_ref, sem):
    del zero_ref
    n = cls_ref.shape[-1]

    def make_copy(r, slot, s):
        return pltpu.make_async_copy(_tile_rows(src_ref, r, 1), _tile_rows(dst_ref, slot, 1), s)

    _slot_copies(pstart_ref, cls_ref, rank_ref, n, make_copy, sem)
    pltpu.make_async_copy(src_ref, _tile_rows(dst_ref, 0, n), sem).wait()


def dispatch_tokens(h2, pstart, cls_blk, rank_blk, cap):
    ts = TOKEN_SUBLANES
    nblk, _, n = cls_blk.shape
    blk = lambda i, ps: (i, 0, 0)
    grid_spec = pltpu.PrefetchScalarGridSpec(
        num_scalar_prefetch=1,
        grid=(nblk,),
        in_specs=[
            pl.BlockSpec((1, 1, n), blk, memory_space=pltpu.SMEM),
            pl.BlockSpec((1, 1, n), blk, memory_space=pltpu.SMEM),
            pl.BlockSpec((n * ts, LANES), lambda i, ps: (i, 0)),
            pl.BlockSpec(memory_space=pl.ANY),
        ],
        out_specs=pl.BlockSpec(memory_space=pl.ANY),
        scratch_shapes=[pltpu.SemaphoreType.DMA(())],
    )
    return pl.pallas_call(
        _dispatch_kernel,
        grid_spec=grid_spec,
        out_shape=jax.ShapeDtypeStruct((cap * ts, LANES), h2.dtype),
        input_output_aliases={4: 0},
        compiler_params=_cparams(("arbitrary",)),
        name="dispatch_tokens",
    )(pstart, cls_blk, rank_blk, h2, jnp.zeros((cap * ts, LANES), h2.dtype))


def _combine_kernel(pstart_ref, cls_ref, rank_ref, tab_ref, x_ref, mod_ref, gf_ref, o_ref, buf_ref, sem, *, final):
    n = x_ref.shape[1]

    def make_copy(r, slot, s):
        return pltpu.make_async_copy(_tile_rows(tab_ref, slot, 1), _tile_rows(buf_ref, r, 1), s)

    _slot_copies(pstart_ref, cls_ref, rank_ref, n, make_copy, sem)
    pltpu.make_async_copy(_tile_rows(tab_ref, 0, n), buf_ref, sem).wait()
    g2 = mod_ref[0][5:6]
    x = x_ref[0] + g2 * _tiles_to_rows(buf_ref, n)
    if final:
        ms = jnp.mean(x * x, axis=-1, keepdims=True)
        x = (x * lax.rsqrt(ms + NORM_EPS)) * gf_ref[...]
    o_ref[0] = x


def combine_residual(xx, ys, pstart, cls_blk, rank_blk, mods, n_ctx_tiles, final_g=None):
    b, lt, d = xx.shape
    tm = TOKEN_TILE
    nj = lt // tm
    final = final_g is not None
    off = n_ctx_tiles if final else 0
    tok = lambda i, j, ps: (i, j + off, 0)
    blk = lambda i, j, ps: (i * nj + j + off, 0, 0)

    def mod_idx(i, j, ps):
        return (jnp.where(j + off < n_ctx_tiles, b, i), 0, 0)

    grid_spec = pltpu.PrefetchScalarGridSpec(
        num_scalar_prefetch=1,
        grid=(b, nj - off),
        in_specs=[
            pl.BlockSpec((1, 1, tm), blk, memory_space=pltpu.SMEM),
            pl.BlockSpec((1, 1, tm), blk, memory_space=pltpu.SMEM),
            pl.BlockSpec(memory_space=pl.ANY),
            pl.BlockSpec((1, tm, d), tok),
            pl.BlockSpec((1, 6, d), mod_idx),
            pl.BlockSpec((1, d), lambda i, j, ps: (0, 0)),
        ],
        out_specs=pl.BlockSpec((1, tm, d), lambda i, j, ps: (i, j, 0)),
        scratch_shapes=[pltpu.VMEM((tm * TOKEN_SUBLANES, LANES), F32), pltpu.SemaphoreType.DMA(())],
    )
    gf = (final_g if final else jnp.ones((d,), F32)).astype(F32).reshape(1, d)
    return pl.pallas_call(
        functools.partial(_combine_kernel, final=final),
        grid_spec=grid_spec,
        out_shape=jax.ShapeDtypeStruct((b, lt - off * tm, d), F32),
        input_output_aliases={} if final else {4: 0},
        compiler_params=_cparams(("arbitrary", "arbitrary")),
        name="combine_final" if final else "combine_residual",
    )(pstart, cls_blk, rank_blk, ys, xx, mods, gf)


def _router_logits(h2, rw_ref):
    n = h2.shape[0]
    hi = h2.astype(BF16)
    lo = (h2 - hi.astype(F32)).astype(BF16)
    pp = _dot(jnp.concatenate([hi, lo], axis=0), rw_ref[...])
    return (pp[0:n, 0:LANES] + pp[0:n, LANES:2 * LANES]) + (pp[n:2 * n, 0:LANES] + pp[n:2 * n, LANES:2 * LANES])


def _expert_kernel(elo_ref, ehi_ref, nv_ref, x_ref, rw_ref, w1a, w3a, w2a, w1b, w3b, w2b, o_ref):
    i = pl.program_id(0)
    n = x_ref.shape[0] // TOKEN_SUBLANES

    @pl.when(i < nv_ref[0])
    def _():
        x = _tiles_to_rows(x_ref, n)
        xb = x.astype(BF16)

        def ffn(w1, w3, w2):
            h1 = _dot(xb, w1[...])
            h3 = _dot(xb, w3[...])
            hid = (h1 * jax.nn.sigmoid(h1)) * h3
            return _dot(hid.astype(BF16), w2[...])

        aff = jax.nn.sigmoid(_router_logits(x, rw_ref))
        lane = lax.broadcasted_iota(jnp.int32, aff.shape, 1)
        a_lo = jnp.sum(jnp.where(lane == elo_ref[i], aff, 0.0), axis=1, keepdims=True)
        a_hi = jnp.sum(jnp.where(lane == ehi_ref[i], aff, 0.0), axis=1, keepdims=True)
        tot = a_lo + a_hi
        y = (a_lo / tot) * ffn(w1a, w3a, w2a) + (a_hi / tot) * ffn(w1b, w3b, w2b)
        for kk in range(TOKEN_SUBLANES):
            o_ref[pl.ds(kk, n, stride=TOKEN_SUBLANES), :] = y[:, kk * LANES:(kk + 1) * LANES]

    @pl.when(i >= nv_ref[0])
    def _():
        o_ref[...] = jnp.zeros_like(o_ref)


def expert_blocks(xs, rw_pad, e_lo, e_hi, n_valid, w1, w3, w2, layer):
    ts = TOKEN_SUBLANES
    cap = xs.shape[0] // ts
    d = ts * LANES
    r = EXPERT_BLOCK
    nblk = cap // r
    de = w1.shape[-1]
    row = lambda i, elo, ehi, nv: (i, 0)
    wlo = lambda i, elo, ehi, nv: (layer, elo[i], 0, 0)
    whi = lambda i, elo, ehi, nv: (layer, ehi[i], 0, 0)
    grid_spec = pltpu.PrefetchScalarGridSpec(
        num_scalar_prefetch=3,
        grid=(nblk,),
        in_specs=[
            pl.BlockSpec((r * ts, LANES), row),
            pl.BlockSpec(rw_pad.shape, lambda i, elo, ehi, nv: (0, 0)),
            pl.BlockSpec((None, None, d, de), wlo),
            pl.BlockSpec((None, None, d, de), wlo),
            pl.BlockSpec((None, None, de, d), wlo),
            pl.BlockSpec((None, None, d, de), whi),
            pl.BlockSpec((None, None, d, de), whi),
            pl.BlockSpec((None, None, de, d), whi),
        ],
        out_specs=pl.BlockSpec((r * ts, LANES), row),
    )
    return pl.pallas_call(
        _expert_kernel,
        grid_spec=grid_spec,
        out_shape=jax.ShapeDtypeStruct((cap * ts, LANES), F32),
        compiler_params=_cparams(("arbitrary",)),
        name="expert_blocks",
    )(e_lo, e_hi, n_valid, xs, rw_pad, w1, w3, w2, w1, w3, w2)


def kernel(x, c, ctx, c_ctx, ada_w, ada_b, norm1_g, norm2_g, final_g, w_in,
           ssm_lam_re, ssm_lam_im, ssm_log_dt, ssm_b_re, ssm_b_im, ssm_c_re, ssm_c_im, ssm_d, w_glu,
           attn_lambda, attn_subln_g, w_attn_out, w_o, router_w, router_b, exp_w1, exp_w3, exp_w2):
    b, s, d = x.shape
    lc = ctx.shape[1]
    depth = w_in.shape[0]
    tm = TOKEN_TILE
    assert lc % tm == 0 and s % tm == 0 and s % GRID_W == 0 and lc % SSM_CHUNK == 0
    n_ctx_tiles = lc // tm
    lt = lc + s

    xx = jnp.concatenate([ctx, x], axis=1)

    n_mod_rows = -(-(b + 1) // 8) * 8
    c_all = jnp.zeros((n_mod_rows, d), F32).at[:b].set(c).at[b].set(c_ctx)
    mods_all = ada_modulation(c_all, ada_w, ada_b).reshape(depth, n_mod_rows, 6, d)

    cos_t, sa_t, sb_t = rope_tables(lc, s)
    assert d == TOKEN_SUBLANES * LANES
    rw_f = jnp.zeros((d, LANES), F32).at[:, :N_EXPERTS].set(router_w.astype(F32))
    rw_hi = rw_f.astype(BF16)
    rw_lo = (rw_f - rw_hi.astype(F32)).astype(BF16)
    rw_pad = jnp.concatenate([rw_hi, rw_lo], axis=1)
    rb = router_b.astype(F32).reshape(N_EXPERTS, 1)

    w_in_bf = w_in.astype(BF16)
    w_glu_bf = w_glu.astype(BF16)
    w_at_bf = w_attn_out.astype(BF16)
    w_o_bf = w_o.astype(BF16)
    e1_bf = exp_w1.astype(BF16)
    e3_bf = exp_w3.astype(BF16)
    e2_bf = exp_w2.astype(BF16)

    w1s, w2s, a16 = jax.vmap(s5_chunk_weights)(ssm_lam_re, ssm_lam_im, ssm_log_dt, ssm_b_re, ssm_b_im,
                                               ssm_c_re, ssm_c_im, ssm_d)

    for l in range(depth):
        lam_init = 0.8 - 0.6 * math.exp(-0.3 * l)
        mods = mods_all[l]
        u, q, k, v, gate = in_projection(xx, mods, norm1_g[l], w_in_bf, l,
                                         cos_t, sa_t, sb_t, n_ctx_tiles)

        yg = s5_mixer(u, w1s, w2s, a16, l, lc)

        lf = attn_lambda[l].astype(F32)
        lam = jnp.exp(jnp.sum(lf[0] * lf[1])) - jnp.exp(jnp.sum(lf[2] * lf[3])) + lam_init
        o = diff_attention(q, k, v, lam, attn_subln_g[l].astype(F32), lc, lam_init)

        xx, h2, ranks, hist = merge_and_route(xx, yg, o, gate, mods, norm2_g[l], w_glu_bf, w_at_bf, w_o_bf, l,
                                              rw_pad, rb, n_ctx_tiles)

        n_tiles = b * lt // tm
        rank_blk = ranks[0].reshape(n_tiles, 1, tm)
        cls_blk = ranks[1].reshape(n_tiles, 1, tm)
        counts = hist[:N_CLASSES, 0].astype(jnp.int32)
        pstart, e_lo, e_hi, n_valid, cap = block_tables(counts, b * lt)
        xs = dispatch_tokens(h2, pstart, cls_blk, rank_blk, cap)
        ys = expert_blocks(xs, rw_pad, e_lo, e_hi, n_valid, e1_bf, e3_bf, e2_bf, l)
        last = l == depth - 1
        xx = combine_residual(xx, ys, pstart, cls_blk, rank_blk, mods, n_ctx_tiles,
                              final_g=final_g if last else None)

    return xx
```

```python
import functools
import math

import jax
import jax.numpy as jnp
from jax import lax
from jax.experimental import pallas as pl
from jax.experimental.pallas import tpu as pltpu

F32 = jnp.float32
BF16 = jnp.bfloat16
HIGHEST = lax.Precision.HIGHEST

GRID_W = 64
SSM_GROUP = 16
SSM_STATE = 64
N_HEADS = 4
QK_HEAD_DIM = 64
V_HEAD_DIM = 128
ROPE_BASE = 10000.0
ROPE_AXIS_FREQS = QK_HEAD_DIM // 4
N_EXPERTS = 16
EXPERTS_PER_GROUP = 4
N_EXPERT_GROUPS = N_EXPERTS // EXPERTS_PER_GROUP
NORM_EPS = 1e-6
SUBLN_EPS = 1e-5

PAIRS = ((0, 1), (0, 2), (0, 3), (1, 2), (1, 3), (2, 3))
N_CLASSES = N_EXPERT_GROUPS * len(PAIRS)

TOKEN_TILE = 256
SSM_CHUNK = 16
EXPERT_BLOCK = 256
VMEM_LIMIT = 56 * 1024 * 1024


def _dot(a, b):
    return jnp.dot(a, b, preferred_element_type=F32)


def _cparams(sem):
    return pltpu.CompilerParams(dimension_semantics=sem, vmem_limit_bytes=VMEM_LIMIT)


def _ada_kernel(c_ref, w_ref, b_ref, o_ref):
    c = c_ref[...]
    s = c * jax.nn.sigmoid(c)
    o_ref[0] = jnp.dot(s, w_ref[0], precision=HIGHEST, preferred_element_type=F32) + b_ref[0]


def ada_modulation(c_all, ada_w, ada_b):
    depth, d, n = ada_w.shape
    r = c_all.shape[0]
    tn = n // 4
    return pl.pallas_call(
        _ada_kernel,
        grid=(depth, n // tn),
        in_specs=[
            pl.BlockSpec((r, d), lambda l, j: (0, 0)),
            pl.BlockSpec((1, d, tn), lambda l, j: (l, 0, j)),
            pl.BlockSpec((1, 1, tn), lambda l, j: (l, 0, j)),
        ],
        out_specs=pl.BlockSpec((1, r, tn), lambda l, j: (l, 0, j)),
        out_shape=jax.ShapeDtypeStruct((depth, r, n), F32),
        compiler_params=_cparams(("arbitrary", "arbitrary")),
        name="ada_modulation",
    )(c_all, ada_w, ada_b.reshape(depth, 1, n))


def _inproj_kernel(x_ref, mod_ref, g_ref, w_ref, cos_ref, sa_ref, sb_ref,
                   u_ref, q_ref, k_ref, v_ref, gate_ref, *, d_ssm, d_qk, d_v):
    _inproj_core(x_ref[0], mod_ref[0], g_ref, w_ref, cos_ref, sa_ref, sb_ref,
                 u_ref, q_ref, k_ref, v_ref, gate_ref, d_ssm, d_qk, d_v)


def _inproj_combine_kernel(pstart_ref, cls_ref, rank_ref, clsn_ref, rankn_ref, ys_ref, x_ref, modp_ref, mod_ref,
                           g_ref, w_ref, cos_ref, sa_ref, sb_ref,
                           xo_ref, u_ref, q_ref, k_ref, v_ref, gate_ref, buf_ref, sem, *, d_ssm, d_qk, d_v):
    n = x_ref.shape[1]
    nj = pl.num_programs(1)
    s = pl.program_id(0) * nj + pl.program_id(1)
    total = pl.num_programs(0) * nj
    slot = lax.rem(s, 2)

    def start_tile(c_ref, r_ref, dst):
        def make_copy(r, tok_slot, sm):
            return pltpu.make_async_copy(_tile_rows(ys_ref, tok_slot, 1), _tile_rows(buf_ref.at[dst], r, 1), sm)
        _slot_copies(pstart_ref, c_ref, r_ref, n, make_copy, sem.at[dst])

    @pl.when(s == 0)
    def _():
        start_tile(cls_ref, rank_ref, 0)

    pltpu.make_async_copy(_tile_rows(ys_ref, 0, n), buf_ref.at[slot], sem.at[slot]).wait()

    @pl.when(s + 1 < total)
    def _():
        start_tile(clsn_ref, rankn_ref, 1 - slot)

    x = x_ref[0] + modp_ref[0][5:6] * _tiles_to_rows(buf_ref.at[slot], n)
    xo_ref[0] = x
    _inproj_core(x, mod_ref[0], g_ref, w_ref, cos_ref, sa_ref, sb_ref,
                 u_ref, q_ref, k_ref, v_ref, gate_ref, d_ssm, d_qk, d_v)


def _inproj_core(x, m, g_ref, w_ref, cos_ref, sa_ref, sb_ref, u_ref, q_ref, k_ref, v_ref, gate_ref,
                 d_ssm, d_qk, d_v):
    ms = jnp.mean(x * x, axis=-1, keepdims=True)
    y = x * lax.rsqrt(ms + NORM_EPS)
    h = (y * g_ref[...]) * (1.0 + m[1:2]) + m[0:1]
    hb = h.astype(BF16)

    cos = cos_ref[...]
    sa = sa_ref[...]
    sb = sb_ref[...]
    half = ROPE_AXIS_FREQS

    def rope(t):
        return t * cos + pltpu.roll(t, d_qk - half, 1) * sa + pltpu.roll(t, half, 1) * sb

    o0 = 0
    u = _dot(hb, w_ref[:, o0:o0 + d_ssm]).astype(BF16)
    c = SSM_CHUNK
    for ci in range(u.shape[0] // c):
        u_ref[ci, 0] = u[ci * c:(ci + 1) * c, :]
    o0 += d_ssm
    q_ref[0] = (rope(_dot(hb, w_ref[:, o0:o0 + d_qk])) * (QK_HEAD_DIM ** -0.5 * math.log2(math.e))).astype(BF16)
    o0 += d_qk
    k_ref[0] = rope(_dot(hb, w_ref[:, o0:o0 + d_qk])).astype(BF16)
    o0 += d_qk
    v_ref[0] = _dot(hb, w_ref[:, o0:o0 + d_v]).astype(BF16)
    o0 += d_v
    n_gate = gate_ref.shape[-1]
    gate_ref[0] = jax.nn.sigmoid(_dot(hb, w_ref[:, o0:o0 + n_gate])).astype(BF16)


def in_projection(xx, mods, g, w_all, layer, cos_t, sa_t, sb_t, n_ctx_tiles, pending=None):
    b, lt, d = xx.shape
    nj = lt // TOKEN_TILE
    n_lat = b
    d_ssm = d // 2
    d_qk = N_HEADS * 2 * QK_HEAD_DIM
    d_v = N_HEADS * V_HEAD_DIM
    n_gate = 2 * d
    tm = TOKEN_TILE
    fused = pending is not None

    def mod_idx(i, j, *_):
        return (jnp.where(j < n_ctx_tiles, n_lat, i), 0, 0)

    tok = lambda i, j, *_: (i, j, 0)
    const2 = lambda i, j, *_: (0, 0)
    table = lambda i, j, *_: (j, 0)
    in_specs = [
        pl.BlockSpec((1, tm, d), tok),
        pl.BlockSpec((1, 6, d), mod_idx),
        pl.BlockSpec((1, d), const2),
        pl.BlockSpec((None,) + w_all.shape[1:], lambda i, j, *_: (layer, 0, 0)),
        pl.BlockSpec((tm, d_qk), table),
        pl.BlockSpec((tm, d_qk), table),
        pl.BlockSpec((tm, d_qk), table),
    ]
    out_specs = [
        pl.BlockSpec((tm // SSM_CHUNK, 1, SSM_CHUNK, d_ssm), lambda i, j, *_: (j, i, 0, 0)),
        pl.BlockSpec((1, tm, d_qk), tok),
        pl.BlockSpec((1, tm, d_qk), tok),
        pl.BlockSpec((1, tm, d_v), tok),
        pl.BlockSpec((1, tm, n_gate), tok),
    ]
    out_shape = [
        jax.ShapeDtypeStruct((lt // SSM_CHUNK, b, SSM_CHUNK, d_ssm), BF16),
        jax.ShapeDtypeStruct((b, lt, d_qk), BF16),
        jax.ShapeDtypeStruct((b, lt, d_qk), BF16),
        jax.ShapeDtypeStruct((b, lt, d_v), BF16),
        jax.ShapeDtypeStruct((b, lt, n_gate), BF16),
    ]
    dims = dict(d_ssm=d_ssm, d_qk=d_qk, d_v=d_v)
    if not fused:
        outs = pl.pallas_call(
            functools.partial(_inproj_kernel, **dims),
            grid=(b, nj),
            in_specs=in_specs,
            out_specs=out_specs,
            out_shape=out_shape,
            compiler_params=_cparams(("arbitrary", "arbitrary")),
            name="in_projection",
        )(xx, mods, g.reshape(1, d), w_all, cos_t, sa_t, sb_t)
        return (xx,) + tuple(outs)

    ys, pstart, cls_blk, rank_blk, mods_prev = pending
    last_tile = b * nj - 1
    this_blk = lambda i, j, ps: (i * nj + j, 0, 0)
    next_blk = lambda i, j, ps: (jnp.minimum(i * nj + j + 1, last_tile), 0, 0)
    smem = lambda f: pl.BlockSpec((1, 1, tm), f, memory_space=pltpu.SMEM)
    grid_spec = pltpu.PrefetchScalarGridSpec(
        num_scalar_prefetch=1,
        grid=(b, nj),
        in_specs=[smem(this_blk), smem(this_blk), smem(next_blk), smem(next_blk),
                  pl.BlockSpec(memory_space=pl.ANY),
                  in_specs[0], pl.BlockSpec((1, 6, d), mod_idx)] + in_specs[1:],
        out_specs=[pl.BlockSpec((1, tm, d), tok)] + out_specs,
        scratch_shapes=[pltpu.VMEM((2, tm * TOKEN_SUBLANES, LANES), F32), pltpu.SemaphoreType.DMA((2,))],
    )
    outs = pl.pallas_call(
        functools.partial(_inproj_combine_kernel, **dims),
        grid_spec=grid_spec,
        out_shape=[jax.ShapeDtypeStruct((b, lt, d), F32)] + out_shape,
        input_output_aliases={6: 0},
        compiler_params=_cparams(("arbitrary", "arbitrary")),
        name="combine_in_projection",
    )(pstart, cls_blk, rank_blk, cls_blk, rank_blk, ys, xx, mods_prev, mods, g.reshape(1, d), w_all,
      cos_t, sa_t, sb_t)
    return tuple(outs)


def rope_tables(lc, s):
    rows = s // GRID_W
    row = jnp.repeat(jnp.arange(rows), GRID_W).astype(F32)
    col = jnp.tile(jnp.arange(GRID_W), rows).astype(F32)
    inv = ROPE_BASE ** (-jnp.arange(ROPE_AXIS_FREQS, dtype=F32) / ROPE_AXIS_FREQS)
    ang_r = row[:, None] * inv
    ang_c = col[:, None] * inv
    ang = jnp.concatenate([ang_r, ang_r, ang_c, ang_c], axis=-1)
    cos = jnp.concatenate([jnp.ones((lc, QK_HEAD_DIM), F32), jnp.cos(ang)], axis=0)
    sin = jnp.concatenate([jnp.zeros((lc, QK_HEAD_DIM), F32), jnp.sin(ang)], axis=0)
    first_half = (jnp.arange(QK_HEAD_DIM) % (2 * ROPE_AXIS_FREQS)) < ROPE_AXIS_FREQS
    sa = jnp.where(first_half, -sin, 0.0)
    sb = jnp.where(first_half, 0.0, sin)
    reps = N_HEADS * 2
    return jnp.tile(cos, (1, reps)), jnp.tile(sa, (1, reps)), jnp.tile(sb, (1, reps))


def _cmul(ar, ai, br, bi):
    return ar * br - ai * bi, ar * bi + ai * br


def s5_chunk_weights(lam_re, lam_im, log_dt, b_re, b_im, c_re, c_im, d_skip):
    c = SSM_CHUNK
    lr = lam_re.astype(F32)
    li = lam_im.astype(F32)
    dt = jnp.exp(log_dt.astype(F32))[..., None]
    decay = jnp.exp(lr * dt)
    ab_re = decay * jnp.cos(li * dt)
    ab_im = decay * jnp.sin(li * dt)
    den = lr * lr + li * li
    nr = ab_re - 1.0
    ni = ab_im
    cr = ((nr * lr + ni * li) / den)[..., None]
    ci = ((ni * lr - nr * li) / den)[..., None]
    br = b_re.astype(F32)
    bi = b_im.astype(F32)
    bb_re = cr * br - ci * bi
    bb_im = cr * bi + ci * br
    ccr = c_re.astype(F32)
    cci = c_im.astype(F32)

    pr = [jnp.ones_like(ab_re)]
    pi = [jnp.zeros_like(ab_im)]
    for _ in range(c):
        r, i = _cmul(pr[-1], pi[-1], ab_re, ab_im)
        pr.append(r)
        pi.append(i)
    pw_re = jnp.stack(pr)
    pw_im = jnp.stack(pi)

    pb_re, pb_im = _cmul(pw_re[..., None], pw_im[..., None], bb_re[None], bb_im[None])
    cp_re, cp_im = _cmul(ccr[None], cci[None], pw_re[:, :, :, None, :], pw_im[:, :, :, None, :])
    kk = (jnp.einsum('dgop,tdgpi->tdgoi', ccr, pb_re[:c], precision=HIGHEST)
          - jnp.einsum('dgop,tdgpi->tdgoi', cci, pb_im[:c], precision=HIGHEST))

    g = lr.shape[1]
    h = SSM_GROUP
    p = SSM_STATE
    ii = jnp.arange(c)[:, None]
    jj = jnp.arange(c)[None, :]
    lags = jnp.arange(c)[:, None, None]
    shift_f = ((jj - ii)[None] == lags).astype(F32)
    shift_b = ((ii - jj)[None] == lags).astype(F32)
    tt = (jnp.einsum('tij,tgon->ginjo', shift_f, kk[:, 0], precision=HIGHEST)
          + jnp.einsum('tij,tgon->ginjo', shift_b, kk[:, 1], precision=HIGHEST))
    dd = d_skip.astype(F32).reshape(g, h)
    eye_t = jnp.eye(c, dtype=F32)
    eye_h = jnp.eye(h, dtype=F32)
    tt = tt + (eye_t[None, :, None, :, None] * eye_h[None, None, :, None, :]
               * dd[:, None, :, None, None])
    tt = tt.reshape(g, c * h, c * h)

    def s_cols(pbx, d, rev):
        m = pbx[:c, d]
        if rev:
            m = m[::-1]
        return jnp.transpose(m, (1, 0, 3, 2)).reshape(g, c * h, p)

    w1 = jnp.concatenate([
        tt,
        s_cols(pb_re, 0, True), s_cols(pb_re, 1, False),
        s_cols(pb_im, 0, True), s_cols(pb_im, 1, False)], axis=-1)

    def o_rows(cpx, d, sign, fwd):
        m = cpx[1:c + 1, d]
        if not fwd:
            m = m[::-1]
        return sign * jnp.transpose(m, (1, 3, 0, 2)).reshape(g, p, c * h)

    w2 = jnp.concatenate([
        o_rows(cp_re, 0, 1.0, True), o_rows(cp_re, 1, 1.0, False),
        o_rows(cp_im, 0, -1.0, True), o_rows(cp_im, 1, -1.0, False)], axis=1)

    a16 = jnp.zeros((g, 8, 2 * p), F32)
    a16 = a16.at[:, 0, :].set(jnp.concatenate([pw_re[c, 0], pw_re[c, 1]], axis=-1))
    a16 = a16.at[:, 1, :].set(jnp.concatenate([pw_im[c, 0], pw_im[c, 1]], axis=-1))
    return w1.astype(BF16), w2.astype(BF16), a16


def _s5_kernel(x_ref, w1_ref, w2_ref, a_ref, y_ref, z_ref, hin_ref, *, nb, nc_ctx, nc_lat):
    ch = x_ref.shape[-1]
    p = SSM_STATE
    z_ref[...] = _dot(x_ref[0], w1_ref[0])
    a_re = a_ref[0, 0:1, :]
    a_im = a_ref[0, 1:2, :]
    fwd_lane = lax.broadcasted_iota(jnp.int32, (nb, 2 * p), 1) < p

    def phase(base, n, carry):
        def step(i, hc):
            h_re, h_im = hc
            rf = pl.multiple_of((base + i) * nb, nb)
            rb = pl.multiple_of((base + n - 1 - i) * nb, nb)
            hin_ref[pl.ds(rf, nb), 0:p] = h_re[:, 0:p]
            hin_ref[pl.ds(rb, nb), p:2 * p] = h_re[:, p:2 * p]
            hin_ref[pl.ds(rf, nb), 2 * p:3 * p] = h_im[:, 0:p]
            hin_ref[pl.ds(rb, nb), 3 * p:4 * p] = h_im[:, p:2 * p]
            s_re = jnp.where(fwd_lane, z_ref[pl.ds(rf, nb), ch:ch + 2 * p],
                             z_ref[pl.ds(rb, nb), ch:ch + 2 * p])
            s_im = jnp.where(fwd_lane, z_ref[pl.ds(rf, nb), ch + 2 * p:ch + 4 * p],
                             z_ref[pl.ds(rb, nb), ch + 2 * p:ch + 4 * p])
            n_re = a_re * h_re - a_im * h_im + s_re
            n_im = a_re * h_im + a_im * h_re + s_im
            return n_re, n_im
        return lax.fori_loop(0, n, step, carry)

    zero = jnp.zeros((nb, 2 * p), F32)
    hc = phase(0, nc_ctx, (zero, zero))
    phase(nc_ctx, nc_lat, hc)
    y = z_ref[:, 0:ch] + _dot(hin_ref[...].astype(BF16), w2_ref[0])
    y_ref[0] = jax.nn.gelu(y).astype(BF16)


def _regroup_kernel(x_ref, o_ref, *, n_groups):
    h = SSM_GROUP
    for t in range(SSM_CHUNK):
        for g in range(n_groups):
            src = (t * n_groups + g) * h
            o_ref[g, :, t * h:(t + 1) * h] = x_ref[:, src:src + h]


def _ungroup_kernel(y_ref, o_ref, *, n_groups):
    h = SSM_GROUP
    for t in range(SSM_CHUNK):
        for g in range(n_groups):
            dst = (t * n_groups + g) * h
            o_ref[:, dst:dst + h] = y_ref[g, :, t * h:(t + 1) * h]


REGROUP_ROWS = 128


def _regroup(x2, n_groups, inverse):
    c = SSM_CHUNK
    h = SSM_GROUP
    if inverse:
        g, m, _ = x2.shape
    else:
        m = x2.shape[0]
        g = n_groups
    rb = min(REGROUP_ROWS, m)
    assert m % rb == 0
    wide = pl.BlockSpec((rb, c * g * h), lambda i: (i, 0))
    grouped = pl.BlockSpec((g, rb, c * h), lambda i: (0, i, 0))
    kern = functools.partial(_ungroup_kernel if inverse else _regroup_kernel, n_groups=g)
    return pl.pallas_call(
        kern,
        grid=(m // rb,),
        in_specs=[grouped if inverse else wide],
        out_specs=wide if inverse else grouped,
        out_shape=jax.ShapeDtypeStruct((m, c * g * h) if inverse else (g, m, c * h), x2.dtype),
        compiler_params=_cparams(("arbitrary",)),
        name="s5_ungroup" if inverse else "s5_regroup",
    )(x2)


def s5_mixer(u4, w1, w2, a16, layer, lc):
    nc, b, c, dssm = u4.shape
    h = SSM_GROUP
    g = dssm // h
    lt = nc * c
    m = nc * b
    ug = _regroup(u4.reshape(m, c * dssm), g, False)
    kern = functools.partial(_s5_kernel, nb=b, nc_ctx=lc // c, nc_lat=(lt - lc) // c)
    yg = pl.pallas_call(
        kern,
        grid=(g,),
        in_specs=[
            pl.BlockSpec((1, m, c * h), lambda i: (i, 0, 0)),
            pl.BlockSpec((None, 1) + w1.shape[2:], lambda i: (layer, i, 0, 0)),
            pl.BlockSpec((None, 1) + w2.shape[2:], lambda i: (layer, i, 0, 0)),
            pl.BlockSpec((None, 1, 8, 2 * SSM_STATE), lambda i: (layer, i, 0, 0)),
        ],
        out_specs=pl.BlockSpec((1, m, c * h), lambda i: (i, 0, 0)),
        out_shape=jax.ShapeDtypeStruct((g, m, c * h), BF16),
        scratch_shapes=[pltpu.VMEM((m, w1.shape[-1]), F32), pltpu.VMEM((m, 4 * SSM_STATE), F32)],
        compiler_params=_cparams(("arbitrary",)),
        name="s5_mixer",
    )(ug, w1, w2, a16)
    return _regroup(yg, g, True).reshape(nc, b, c, dssm)


def _attn_kernel(lam_ref, q_ref, k_ref, v_ref, g_ref, o_ref, *, n_ctx_tiles, lc, out_scale):
    j = pl.program_id(2)
    lam = lam_ref[0]
    q = q_ref[0]
    lo = lax.broadcasted_iota(jnp.int32, q.shape, 1) < QK_HEAD_DIM
    zq = jnp.zeros_like(q)
    q1 = jnp.where(lo, q, zq)
    q2 = jnp.where(lo, zq, q)
    nt = (((1,), (1,)), ((), ()))

    def run(nk):
        k = k_ref[0, 0:nk, :]
        v = v_ref[0, 0:nk, :]
        s1 = lax.dot_general(q1, k, nt, preferred_element_type=F32)
        s2 = lax.dot_general(q2, k, nt, preferred_element_type=F32)
        e1 = jnp.exp2(s1 - jnp.max(s1, axis=-1, keepdims=True))
        e2 = jnp.exp2(s2 - jnp.max(s2, axis=-1, keepdims=True))
        l1 = jnp.sum(e1, axis=-1, keepdims=True)
        l2 = jnp.sum(e2, axis=-1, keepdims=True)
        a = e1 - e2 * (lam * l1 / l2)
        o = _dot(a.astype(BF16), v) * (1.0 / l1)
        ms = jnp.mean(o * o, axis=-1, keepdims=True)
        o = (o * lax.rsqrt(ms + SUBLN_EPS)) * g_ref[...] * out_scale
        o_ref[0] = o.astype(BF16)

    @pl.when(j < n_ctx_tiles)
    def _():
        run(lc)

    @pl.when(j >= n_ctx_tiles)
    def _():
        run(k_ref.shape[1])


def diff_attention(q, k, v, lam, subln_g, lc, lam_init):
    b, lt, _ = q.shape
    tm = TOKEN_TILE
    nj = lt // tm
    hd = 2 * QK_HEAD_DIM
    kern = functools.partial(_attn_kernel, n_ctx_tiles=lc // tm, lc=lc, out_scale=1.0 - lam_init)
    return pl.pallas_call(
        kern,
        grid=(b, N_HEADS, nj),
        in_specs=[
            pl.BlockSpec(memory_space=pltpu.SMEM),
            pl.BlockSpec((1, tm, hd), lambda i, h, j: (i, j, h)),
            pl.BlockSpec((1, lt, hd), lambda i, h, j: (i, 0, h)),
            pl.BlockSpec((1, lt, V_HEAD_DIM), lambda i, h, j: (i, 0, h)),
            pl.BlockSpec((1, V_HEAD_DIM), lambda i, h, j: (0, 0)),
        ],
        out_specs=pl.BlockSpec((1, tm, V_HEAD_DIM), lambda i, h, j: (i, j, h)),
        out_shape=jax.ShapeDtypeStruct((b, lt, N_HEADS * V_HEAD_DIM), BF16),
        compiler_params=_cparams(("arbitrary", "arbitrary", "arbitrary")),
        name="diff_attention",
    )(lam.reshape(1), q, k, v, subln_g.reshape(1, V_HEAD_DIM))


def _route(logits_t, rb):
    aff = jax.nn.sigmoid(logits_t)
    sel = aff + rb
    s = [sel[e:e + 1] for e in range(N_EXPERTS)]
    npg = EXPERTS_PER_GROUP
    gscore = []
    for g in range(N_EXPERT_GROUPS):
        best = None
        for (i, j) in PAIRS:
            ps = s[g * npg + i] + s[g * npg + j]
            best = ps if best is None else jnp.maximum(best, ps)
        gscore.append(best)
    g_sel = jnp.zeros_like(gscore[0], dtype=jnp.int32)
    g_best = gscore[0]
    for g in range(1, N_EXPERT_GROUPS):
        take = gscore[g] > g_best
        g_sel = jnp.where(take, g, g_sel)
        g_best = jnp.where(take, gscore[g], g_best)

    def pick(rows, i):
        out = rows[i]
        for g in range(1, N_EXPERT_GROUPS):
            out = jnp.where(g_sel == g, rows[g * npg + i], out)
        return out

    xs = [pick(s, i) for i in range(npg)]
    i1 = jnp.zeros_like(g_sel)
    v1 = xs[0]
    for i in range(1, npg):
        take = xs[i] > v1
        i1 = jnp.where(take, i, i1)
        v1 = jnp.where(take, xs[i], v1)
    i2 = jnp.full_like(g_sel, -1)
    v2 = jnp.full_like(v1, -jnp.inf)
    for i in range(npg):
        take = (i1 != i) & ((i2 < 0) | (xs[i] > v2))
        i2 = jnp.where(take, i, i2)
        v2 = jnp.where(take, xs[i], v2)
    lo = jnp.minimum(i1, i2)
    hi = jnp.maximum(i1, i2)
    pair = jnp.zeros_like(lo)
    for pi_, (i, j) in enumerate(PAIRS):
        pair = jnp.where((lo == i) & (hi == j), pi_, pair)
    return g_sel * len(PAIRS) + pair


def _merge_kernel(x_ref, yg_ref, o_ref, gate_ref, mod_ref, g2_ref, wglu_ref, wat_ref, wo_ref,
                  rw_ref, rb_ref, xo_ref, h2_ref, route_ref, cnt_ref):
    d = x_ref.shape[-1]
    tm = x_ref.shape[1]
    m = mod_ref[0]
    yg = jnp.concatenate([yg_ref[ci, 0] for ci in range(yg_ref.shape[0])], axis=0)
    ab = _dot(yg, wglu_ref[...])
    p_s = ab[:, 0:d] * jax.nn.sigmoid(ab[:, d:2 * d])
    p_a = _dot(o_ref[0], wat_ref[...])
    gate = gate_ref[0]
    mix = gate[:, 0:d].astype(F32) * p_s + gate[:, d:2 * d].astype(F32) * p_a
    x = x_ref[0] + m[2:3] * _dot(mix.astype(BF16), wo_ref[...])
    xo_ref[0] = x
    ms = jnp.mean(x * x, axis=-1, keepdims=True)
    y = x * lax.rsqrt(ms + NORM_EPS)
    h2 = (y * g2_ref[...]) * (1.0 + m[4:5]) + m[3:4]
    lanes = 128
    for kk in range(d // lanes):
        h2_ref[pl.ds(kk, tm, stride=d // lanes), :] = h2[:, kk * lanes:(kk + 1) * lanes]
    lt = _router_logits(h2, rw_ref).T[0:N_EXPERTS]
    cls = _route(lt, rb_ref[...])

    @pl.when((pl.program_id(0) == 0) & (pl.program_id(1) == 0))
    def _():
        cnt_ref[...] = jnp.zeros_like(cnt_ref)

    member = lax.broadcasted_iota(jnp.int32, (cnt_ref.shape[0], tm), 0) == cls
    earlier = (lax.broadcasted_iota(jnp.int32, (tm, tm), 0) < lax.broadcasted_iota(jnp.int32, (tm, tm), 1))
    before = _dot(member.astype(BF16), earlier.astype(BF16))
    rank = jnp.sum(jnp.where(member, before + cnt_ref[:, 0:1], 0.0), axis=0, keepdims=True)
    cnt_ref[...] = cnt_ref[...] + jnp.sum(member.astype(F32), axis=1, keepdims=True)
    route_ref[...] = jnp.concatenate([rank.astype(jnp.int32), cls, jnp.zeros((6, tm), jnp.int32)], axis=0)


def merge_and_route(xx, yg, o, gate, mods, g2, wglu, wat, wo, layer, rw_pad, rb, n_ctx_tiles):
    b, lt, d = xx.shape
    tm = TOKEN_TILE
    nj = lt // tm
    t = b * lt
    tok = lambda i, j: (i, j, 0)
    const = lambda i, j: (0, 0)

    def mod_idx(i, j):
        return (jnp.where(j < n_ctx_tiles, b, i), 0, 0)

    return pl.pallas_call(
        _merge_kernel,
        grid=(b, nj),
        in_specs=[
            pl.BlockSpec((1, tm, d), tok),
            pl.BlockSpec((tm // SSM_CHUNK, 1, SSM_CHUNK, yg.shape[-1]), lambda i, j: (j, i, 0, 0)),
            pl.BlockSpec((1, tm, o.shape[-1]), tok),
            pl.BlockSpec((1, tm, gate.shape[-1]), tok),
            pl.BlockSpec((1, 6, d), mod_idx),
            pl.BlockSpec((1, d), const),
            pl.BlockSpec((None,) + wglu.shape[1:], lambda i, j: (layer, 0, 0)),
            pl.BlockSpec((None,) + wat.shape[1:], lambda i, j: (layer, 0, 0)),
            pl.BlockSpec((None,) + wo.shape[1:], lambda i, j: (layer, 0, 0)),
            pl.BlockSpec(rw_pad.shape, const),
            pl.BlockSpec(rb.shape, const),
        ],
        out_specs=[
            pl.BlockSpec((1, tm, d), tok),
            pl.BlockSpec((tm * (d // 128), 128), lambda i, j: (i * nj + j, 0)),
            pl.BlockSpec((8, tm), lambda i, j: (0, i * nj + j)),
            pl.BlockSpec((LANES, LANES), const),
        ],
        out_shape=[
            jax.ShapeDtypeStruct((b, lt, d), F32),
            jax.ShapeDtypeStruct((t * (d // 128), 128), F32),
            jax.ShapeDtypeStruct((8, t), jnp.int32),
            jax.ShapeDtypeStruct((LANES, LANES), F32),
        ],
        input_output_aliases={0: 0},
        compiler_params=_cparams(("arbitrary", "arbitrary")),
        name="merge_and_route",
    )(xx, yg, o, gate, mods, g2.reshape(1, d), wglu, wat, wo, rw_pad, rb)


def block_tables(counts, t):
    r = EXPERT_BLOCK
    ncls = N_CLASSES
    cap = -(-(t + ncls * (r - 1)) // r) * r
    nblk = cap // r
    padded = (counts + r - 1) // r * r
    pend = jnp.cumsum(padded)
    pstart = pend - padded
    n_valid = pend[-1] // r
    blk = jnp.arange(nblk, dtype=jnp.int32)
    blk_cls = jnp.minimum(jnp.sum((pend[None, :] <= (blk * r)[:, None]).astype(jnp.int32), axis=1), ncls - 1)
    last_cls = jnp.max(jnp.where(counts > 0, jnp.arange(ncls, dtype=jnp.int32), 0))
    blk_cls = jnp.where(blk < n_valid, blk_cls, last_cls)
    pair_lo = jnp.array([p[0] for p in PAIRS], jnp.int32)
    pair_hi = jnp.array([p[1] for p in PAIRS], jnp.int32)
    grp = blk_cls // len(PAIRS)
    pr = blk_cls % len(PAIRS)
    e_lo = grp * EXPERTS_PER_GROUP + pair_lo[pr]
    e_hi = grp * EXPERTS_PER_GROUP + pair_hi[pr]
    pstart_pad = jnp.pad(pstart.astype(jnp.int32), (0, 32 - ncls))
    return pstart_pad, e_lo, e_hi, n_valid.astype(jnp.int32).reshape(1), cap


TOKEN_SUBLANES = 8
LANES = 128
GATHER_UNROLL = 8


def _slot_copies(pstart_ref, cls_ref, rank_ref, n, make_copy, sem):
    assert n % GATHER_UNROLL == 0

    def issue(i, carry):
        for uu in range(GATHER_UNROLL):
            r = i * GATHER_UNROLL + uu
            slot = pstart_ref[cls_ref[0, 0, r]] + rank_ref[0, 0, r]
            make_copy(r, slot, sem).start(priority=uu % 2)
        return carry
    lax.fori_loop(0, n // GATHER_UNROLL, issue, 0)


def _tile_rows(ref, first, n):
    ts = TOKEN_SUBLANES
    return ref.at[pl.ds(pl.multiple_of(first * ts, ts), n * ts)]


def _tiles_to_rows(ref, n):
    ts = TOKEN_SUBLANES
    return jnp.concatenate([ref[pl.ds(kk, n, stride=ts), :] for kk in range(ts)], axis=1)


def _dispatch_kernel(pstart_ref, cls_ref, rank_ref, src_ref, zero_ref, dst_ref, sem):
    del zero_ref
    n = cls_ref.shape[-1]

    def make_copy(r, slot, s):
        return pltpu.make_async_copy(_tile_rows(src_ref, r, 1), _tile_rows(dst_ref, slot, 1), s)

    _slot_copies(pstart_ref, cls_ref, rank_ref, n, make_copy, sem)
    pltpu.make_async_copy(src_ref, _tile_rows(dst_ref, 0, n), sem).wait()


def dispatch_tokens(h2, pstart, cls_blk, rank_blk, cap, init):
    ts = TOKEN_SUBLANES
    nblk, _, n = cls_blk.shape
    blk = lambda i, ps: (i, 0, 0)
    grid_spec = pltpu.PrefetchScalarGridSpec(
        num_scalar_prefetch=1,
        grid=(nblk,),
        in_specs=[
            pl.BlockSpec((1, 1, n), blk, memory_space=pltpu.SMEM),
            pl.BlockSpec((1, 1, n), blk, memory_space=pltpu.SMEM),
            pl.BlockSpec((n * ts, LANES), lambda i, ps: (i, 0)),
            pl.BlockSpec(memory_space=pl.ANY),
        ],
        out_specs=pl.BlockSpec(memory_space=pl.ANY),
        scratch_shapes=[pltpu.SemaphoreType.DMA(())],
    )
    return pl.pallas_call(
        _dispatch_kernel,
        grid_spec=grid_spec,
        out_shape=jax.ShapeDtypeStruct((cap * ts, LANES), h2.dtype),
        input_output_aliases={4: 0},
        compiler_params=_cparams(("arbitrary",)),
        name="dispatch_tokens",
    )(pstart, cls_blk, rank_blk, h2, init)


def _combine_kernel(pstart_ref, cls_ref, rank_ref, tab_ref, x_ref, mod_ref, gf_ref, o_ref, buf_ref, sem, *, final):
    n = x_ref.shape[1]

    def make_copy(r, slot, s):
        return pltpu.make_async_copy(_tile_rows(tab_ref, slot, 1), _tile_rows(buf_ref, r, 1), s)

    _slot_copies(pstart_ref, cls_ref, rank_ref, n, make_copy, sem)
    pltpu.make_async_copy(_tile_rows(tab_ref, 0, n), buf_ref, sem).wait()
    g2 = mod_ref[0][5:6]
    x = x_ref[0] + g2 * _tiles_to_rows(buf_ref, n)
    if final:
        ms = jnp.mean(x * x, axis=-1, keepdims=True)
        x = (x * lax.rsqrt(ms + NORM_EPS)) * gf_ref[...]
    o_ref[0] = x


def combine_residual(xx, ys, pstart, cls_blk, rank_blk, mods, n_ctx_tiles, final_g=None):
    b, lt, d = xx.shape
    tm = TOKEN_TILE
    nj = lt // tm
    final = final_g is not None
    off = n_ctx_tiles if final else 0
    tok = lambda i, j, ps: (i, j + off, 0)
    blk = lambda i, j, ps: (i * nj + j + off, 0, 0)

    def mod_idx(i, j, ps):
        return (jnp.where(j + off < n_ctx_tiles, b, i), 0, 0)

    grid_spec = pltpu.PrefetchScalarGridSpec(
        num_scalar_prefetch=1,
        grid=(b, nj - off),
        in_specs=[
            pl.BlockSpec((1, 1, tm), blk, memory_space=pltpu.SMEM),
            pl.BlockSpec((1, 1, tm), blk, memory_space=pltpu.SMEM),
            pl.BlockSpec(memory_space=pl.ANY),
            pl.BlockSpec((1, tm, d), tok),
            pl.BlockSpec((1, 6, d), mod_idx),
            pl.BlockSpec((1, d), lambda i, j, ps: (0, 0)),
        ],
        out_specs=pl.BlockSpec((1, tm, d), lambda i, j, ps: (i, j, 0)),
        scratch_shapes=[pltpu.VMEM((tm * TOKEN_SUBLANES, LANES), F32), pltpu.SemaphoreType.DMA(())],
    )
    gf = (final_g if final else jnp.ones((d,), F32)).astype(F32).reshape(1, d)
    return pl.pallas_call(
        functools.partial(_combine_kernel, final=final),
        grid_spec=grid_spec,
        out_shape=jax.ShapeDtypeStruct((b, lt - off * tm, d), F32),
        input_output_aliases={} if final else {4: 0},
        compiler_params=_cparams(("arbitrary", "arbitrary")),
        name="combine_final" if final else "combine_residual",
    )(pstart, cls_blk, rank_blk, ys, xx, mods, gf)


def _router_logits(h2, rw_ref):
    n = h2.shape[0]
    hi = h2.astype(BF16)
    lo = (h2 - hi.astype(F32)).astype(BF16)
    pp = _dot(jnp.concatenate([hi, lo], axis=0), rw_ref[...])
    return (pp[0:n, 0:LANES] + pp[0:n, LANES:2 * LANES]) + (pp[n:2 * n, 0:LANES] + pp[n:2 * n, LANES:2 * LANES])


def _expert_kernel(elo_ref, ehi_ref, nv_ref, x_ref, rw_ref, w1a, w3a, w2a, w1b, w3b, w2b, o_ref):
    i = pl.program_id(0)
    n = x_ref.shape[0] // TOKEN_SUBLANES

    @pl.when(i < nv_ref[0])
    def _():
        x = _tiles_to_rows(x_ref, n)
        xb = x.astype(BF16)

        def ffn(w1, w3, w2):
            h1 = _dot(xb, w1[...])
            h3 = _dot(xb, w3[...])
            hid = (h1 * jax.nn.sigmoid(h1)) * h3
            return _dot(hid.astype(BF16), w2[...])

        aff = jax.nn.sigmoid(_router_logits(x, rw_ref))
        lane = lax.broadcasted_iota(jnp.int32, aff.shape, 1)
        a_lo = jnp.sum(jnp.where(lane == elo_ref[i], aff, 0.0), axis=1, keepdims=True)
        a_hi = jnp.sum(jnp.where(lane == ehi_ref[i], aff, 0.0), axis=1, keepdims=True)
        tot = a_lo + a_hi
        y = (a_lo / tot) * ffn(w1a, w3a, w2a) + (a_hi / tot) * ffn(w1b, w3b, w2b)
        for kk in range(TOKEN_SUBLANES):
            o_ref[pl.ds(kk, n, stride=TOKEN_SUBLANES), :] = y[:, kk * LANES:(kk + 1) * LANES]

    @pl.when(i >= nv_ref[0])
    def _():
        o_ref[...] = jnp.zeros_like(o_ref)


def expert_blocks(xs, rw_pad, e_lo, e_hi, n_valid, w1, w3, w2, layer):
    ts = TOKEN_SUBLANES
    cap = xs.shape[0] // ts
    d = ts * LANES
    r = EXPERT_BLOCK
    nblk = cap // r
    de = w1.shape[-1]
    row = lambda i, elo, ehi, nv: (i, 0)
    wlo = lambda i, elo, ehi, nv: (layer, elo[i], 0, 0)
    whi = lambda i, elo, ehi, nv: (layer, ehi[i], 0, 0)
    grid_spec = pltpu.PrefetchScalarGridSpec(
        num_scalar_prefetch=3,
        grid=(nblk,),
        in_specs=[
            pl.BlockSpec((r * ts, LANES), row),
            pl.BlockSpec(rw_pad.shape, lambda i, elo, ehi, nv: (0, 0)),
            pl.BlockSpec((None, None, d, de), wlo),
            pl.BlockSpec((None, None, d, de), wlo),
            pl.BlockSpec((None, None, de, d), wlo),
            pl.BlockSpec((None, None, d, de), whi),
            pl.BlockSpec((None, None, d, de), whi),
            pl.BlockSpec((None, None, de, d), whi),
        ],
        out_specs=pl.BlockSpec((r * ts, LANES), row),
    )
    return pl.pallas_call(
        _expert_kernel,
        grid_spec=grid_spec,
        out_shape=jax.ShapeDtypeStruct((cap * ts, LANES), F32),
        compiler_params=_cparams(("arbitrary",)),
        name="expert_blocks",
    )(e_lo, e_hi, n_valid, xs, rw_pad, w1, w3, w2, w1, w3, w2)


def kernel(x, c, ctx, c_ctx, ada_w, ada_b, norm1_g, norm2_g, final_g, w_in,
           ssm_lam_re, ssm_lam_im, ssm_log_dt, ssm_b_re, ssm_b_im, ssm_c_re, ssm_c_im, ssm_d, w_glu,
           attn_lambda, attn_subln_g, w_attn_out, w_o, router_w, router_b, exp_w1, exp_w3, exp_w2):
    b, s, d = x.shape
    lc = ctx.shape[1]
    depth = w_in.shape[0]
    tm = TOKEN_TILE
    assert lc % tm == 0 and s % tm == 0 and s % GRID_W == 0 and lc % SSM_CHUNK == 0
    n_ctx_tiles = lc // tm
    lt = lc + s

    xx = jnp.concatenate([ctx, x], axis=1)

    n_mod_rows = -(-(b + 1) // 8) * 8
    c_all = jnp.zeros((n_mod_rows, d), F32).at[:b].set(c).at[b].set(c_ctx)
    mods_all = ada_modulation(c_all, ada_w, ada_b).reshape(depth, n_mod_rows, 6, d)

    cos_t, sa_t, sb_t = rope_tables(lc, s)
    assert d == TOKEN_SUBLANES * LANES
    rw_f = jnp.zeros((d, LANES), F32).at[:, :N_EXPERTS].set(router_w.astype(F32))
    rw_hi = rw_f.astype(BF16)
    rw_lo = (rw_f - rw_hi.astype(F32)).astype(BF16)
    rw_pad = jnp.concatenate([rw_hi, rw_lo], axis=1)
    rb = router_b.astype(F32).reshape(N_EXPERTS, 1)

    w_in_bf = w_in.astype(BF16)
    w_glu_bf = w_glu.astype(BF16)
    w_at_bf = w_attn_out.astype(BF16)
    w_o_bf = w_o.astype(BF16)
    e1_bf = exp_w1.astype(BF16)
    e3_bf = exp_w3.astype(BF16)
    e2_bf = exp_w2.astype(BF16)

    w1s, w2s, a16 = jax.vmap(s5_chunk_weights)(ssm_lam_re, ssm_lam_im, ssm_log_dt, ssm_b_re, ssm_b_im,
                                               ssm_c_re, ssm_c_im, ssm_d)

    pending = None
    xs = None
    for l in range(depth):
        lam_init = 0.8 - 0.6 * math.exp(-0.3 * l)
        mods = mods_all[l]
        xx, u, q, k, v, gate = in_projection(xx, mods, norm1_g[l], w_in_bf, l,
                                             cos_t, sa_t, sb_t, n_ctx_tiles, pending)

        yg = s5_mixer(u, w1s, w2s, a16, l, lc)

        lf = attn_lambda[l].astype(F32)
        lam = jnp.exp(jnp.sum(lf[0] * lf[1])) - jnp.exp(jnp.sum(lf[2] * lf[3])) + lam_init
        o = diff_attention(q, k, v, lam, attn_subln_g[l].astype(F32), lc, lam_init)

        xx, h2, ranks, hist = merge_and_route(xx, yg, o, gate, mods, norm2_g[l], w_glu_bf, w_at_bf, w_o_bf, l,
                                              rw_pad, rb, n_ctx_tiles)

        n_tiles = b * lt // tm
        rank_blk = ranks[0].reshape(n_tiles, 1, tm)
        cls_blk = ranks[1].reshape(n_tiles, 1, tm)
        counts = hist[:N_CLASSES, 0].astype(jnp.int32)
        pstart, e_lo, e_hi, n_valid, cap = block_tables(counts, b * lt)
        init = jnp.zeros((cap * TOKEN_SUBLANES, LANES), F32) if xs is None else xs
        xs = dispatch_tokens(h2, pstart, cls_blk, rank_blk, cap, init)
        ys = expert_blocks(xs, rw_pad, e_lo, e_hi, n_valid, e1_bf, e3_bf, e2_bf, l)
        pending = (ys, pstart, cls_blk, rank_blk, mods)

    ys, pstart, cls_blk, rank_blk, mods = pending
    return combine_residual(xx, ys, pstart, cls_blk, rank_blk, mods, n_ctx_tiles, final_g=final_g)
```

```python
import functools
import math

import jax
import jax.numpy as jnp
from jax import lax
from jax.experimental import pallas as pl
from jax.experimental.pallas import tpu as pltpu

F32 = jnp.float32
BF16 = jnp.bfloat16
HIGHEST = lax.Precision.HIGHEST

GRID_W = 64
SSM_GROUP = 16
SSM_STATE = 64
N_HEADS = 4
QK_HEAD_DIM = 64
V_HEAD_DIM = 128
ROPE_BASE = 10000.0
ROPE_AXIS_FREQS = QK_HEAD_DIM // 4
N_EXPERTS = 16
EXPERTS_PER_GROUP = 4
N_EXPERT_GROUPS = N_EXPERTS // EXPERTS_PER_GROUP
NORM_EPS = 1e-6
SUBLN_EPS = 1e-5

PAIRS = ((0, 1), (0, 2), (0, 3), (1, 2), (1, 3), (2, 3))
N_CLASSES = N_EXPERT_GROUPS * len(PAIRS)

TOKEN_TILE = 256
SSM_CHUNK = 16
EXPERT_BLOCK = 256
VMEM_LIMIT = 56 * 1024 * 1024


def _dot(a, b):
    return jnp.dot(a, b, preferred_element_type=F32)


def _cparams(sem):
    return pltpu.CompilerParams(dimension_semantics=sem, vmem_limit_bytes=VMEM_LIMIT)


def _ada_kernel(c_ref, w_ref, b_ref, o_ref):
    c = c_ref[...]
    s = c * jax.nn.sigmoid(c)
    o_ref[0] = jnp.dot(s, w_ref[0], precision=HIGHEST, preferred_element_type=F32) + b_ref[0]


def ada_modulation(c_all, ada_w, ada_b):
    depth, d, n = ada_w.shape
    r = c_all.shape[0]
    tn = n // 4
    return pl.pallas_call(
        _ada_kernel,
        grid=(depth, n // tn),
        in_specs=[
            pl.BlockSpec((r, d), lambda l, j: (0, 0)),
            pl.BlockSpec((1, d, tn), lambda l, j: (l, 0, j)),
            pl.BlockSpec((1, 1, tn), lambda l, j: (l, 0, j)),
        ],
        out_specs=pl.BlockSpec((1, r, tn), lambda l, j: (l, 0, j)),
        out_shape=jax.ShapeDtypeStruct((depth, r, n), F32),
        compiler_params=_cparams(("arbitrary", "arbitrary")),
        name="ada_modulation",
    )(c_all, ada_w, ada_b.reshape(depth, 1, n))


def _inproj_kernel(x_ref, mod_ref, g_ref, w_ref, cos_ref, sa_ref, sb_ref,
                   u_ref, q_ref, k_ref, v_ref, gate_ref, *, d_ssm, d_qk, d_v):
    _inproj_core(x_ref[0], mod_ref[0], g_ref, w_ref, cos_ref, sa_ref, sb_ref,
                 u_ref, q_ref, k_ref, v_ref, gate_ref, d_ssm, d_qk, d_v)


def _inproj_combine_kernel(pstart_ref, cls_ref, rank_ref, clsn_ref, rankn_ref, ys_ref, x_ref, modp_ref, mod_ref,
                           g_ref, w_ref, cos_ref, sa_ref, sb_ref,
                           xo_ref, u_ref, q_ref, k_ref, v_ref, gate_ref, buf_ref, sem, *, d_ssm, d_qk, d_v):
    n = x_ref.shape[1]
    nj = pl.num_programs(1)
    s = pl.program_id(0) * nj + pl.program_id(1)
    total = pl.num_programs(0) * nj
    slot = lax.rem(s, 2)

    def start_tile(c_ref, r_ref, dst):
        def make_copy(r, tok_slot, sm):
            return pltpu.make_async_copy(_tile_rows(ys_ref, tok_slot, 1), _tile_rows(buf_ref.at[dst], r, 1), sm)
        _slot_copies(pstart_ref, c_ref, r_ref, n, make_copy, sem.at[dst])

    @pl.when(s == 0)
    def _():
        start_tile(cls_ref, rank_ref, 0)

    pltpu.make_async_copy(_tile_rows(ys_ref, 0, n), buf_ref.at[slot], sem.at[slot]).wait()

    @pl.when(s + 1 < total)
    def _():
        start_tile(clsn_ref, rankn_ref, 1 - slot)

    x = x_ref[0] + modp_ref[0][5:6] * _tiles_to_rows(buf_ref.at[slot], n)
    xo_ref[0] = x
    _inproj_core(x, mod_ref[0], g_ref, w_ref, cos_ref, sa_ref, sb_ref,
                 u_ref, q_ref, k_ref, v_ref, gate_ref, d_ssm, d_qk, d_v)


def _inproj_core(x, m, g_ref, w_ref, cos_ref, sa_ref, sb_ref, u_ref, q_ref, k_ref, v_ref, gate_ref,
                 d_ssm, d_qk, d_v):
    ms = jnp.mean(x * x, axis=-1, keepdims=True)
    y = x * lax.rsqrt(ms + NORM_EPS)
    h = (y * g_ref[...]) * (1.0 + m[1:2]) + m[0:1]
    hb = h.astype(BF16)

    cos = cos_ref[...]
    sa = sa_ref[...]
    sb = sb_ref[...]
    half = ROPE_AXIS_FREQS

    def rope(t):
        return t * cos + pltpu.roll(t, d_qk - half, 1) * sa + pltpu.roll(t, half, 1) * sb

    o0 = 0
    u = _dot(hb, w_ref[:, o0:o0 + d_ssm]).astype(BF16)
    c = SSM_CHUNK
    for ci in range(u.shape[0] // c):
        u_ref[ci, 0] = u[ci * c:(ci + 1) * c, :]
    o0 += d_ssm
    q_ref[0] = (rope(_dot(hb, w_ref[:, o0:o0 + d_qk])) * (QK_HEAD_DIM ** -0.5 * math.log2(math.e))).astype(BF16)
    o0 += d_qk
    k_ref[0] = rope(_dot(hb, w_ref[:, o0:o0 + d_qk])).astype(BF16)
    o0 += d_qk
    v_ref[0] = _dot(hb, w_ref[:, o0:o0 + d_v]).astype(BF16)
    o0 += d_v
    n_gate = gate_ref.shape[-1]
    gate_ref[0] = jax.nn.sigmoid(_dot(hb, w_ref[:, o0:o0 + n_gate])).astype(BF16)


def in_projection(xx, mods, g, w_all, layer, cos_t, sa_t, sb_t, n_ctx_tiles, pending=None):
    b, lt, d = xx.shape
    nj = lt // TOKEN_TILE
    n_lat = b
    d_ssm = d // 2
    d_qk = N_HEADS * 2 * QK_HEAD_DIM
    d_v = N_HEADS * V_HEAD_DIM
    n_gate = 2 * d
    tm = TOKEN_TILE
    fused = pending is not None

    def mod_idx(i, j, *_):
        return (jnp.where(j < n_ctx_tiles, n_lat, i), 0, 0)

    tok = lambda i, j, *_: (i, j, 0)
    const2 = lambda i, j, *_: (0, 0)
    table = lambda i, j, *_: (j, 0)
    in_specs = [
        pl.BlockSpec((1, tm, d), tok),
        pl.BlockSpec((1, 6, d), mod_idx),
        pl.BlockSpec((1, d), const2),
        pl.BlockSpec((None,) + w_all.shape[1:], lambda i, j, *_: (layer, 0, 0)),
        pl.BlockSpec((tm, d_qk), table),
        pl.BlockSpec((tm, d_qk), table),
        pl.BlockSpec((tm, d_qk), table),
    ]
    out_specs = [
        pl.BlockSpec((tm // SSM_CHUNK, 1, SSM_CHUNK, d_ssm), lambda i, j, *_: (j, i, 0, 0)),
        pl.BlockSpec((1, tm, d_qk), tok),
        pl.BlockSpec((1, tm, d_qk), tok),
        pl.BlockSpec((1, tm, d_v), tok),
        pl.BlockSpec((1, tm, n_gate), tok),
    ]
    out_shape = [
        jax.ShapeDtypeStruct((lt // SSM_CHUNK, b, SSM_CHUNK, d_ssm), BF16),
        jax.ShapeDtypeStruct((b, lt, d_qk), BF16),
        jax.ShapeDtypeStruct((b, lt, d_qk), BF16),
        jax.ShapeDtypeStruct((b, lt, d_v), BF16),
        jax.ShapeDtypeStruct((b, lt, n_gate), BF16),
    ]
    dims = dict(d_ssm=d_ssm, d_qk=d_qk, d_v=d_v)
    if not fused:
        outs = pl.pallas_call(
            functools.partial(_inproj_kernel, **dims),
            grid=(b, nj),
            in_specs=in_specs,
            out_specs=out_specs,
            out_shape=out_shape,
            compiler_params=_cparams(("arbitrary", "arbitrary")),
            name="in_projection",
        )(xx, mods, g.reshape(1, d), w_all, cos_t, sa_t, sb_t)
        return (xx,) + tuple(outs)

    ys, pstart, cls_blk, rank_blk, mods_prev = pending
    last_tile = b * nj - 1
    this_blk = lambda i, j, ps: (i * nj + j, 0, 0)
    next_blk = lambda i, j, ps: (jnp.minimum(i * nj + j + 1, last_tile), 0, 0)
    smem = lambda f: pl.BlockSpec((1, 1, tm), f, memory_space=pltpu.SMEM)
    grid_spec = pltpu.PrefetchScalarGridSpec(
        num_scalar_prefetch=1,
        grid=(b, nj),
        in_specs=[smem(this_blk), smem(this_blk), smem(next_blk), smem(next_blk),
                  pl.BlockSpec(memory_space=pl.ANY),
                  in_specs[0], pl.BlockSpec((1, 6, d), mod_idx)] + in_specs[1:],
        out_specs=[pl.BlockSpec((1, tm, d), tok)] + out_specs,
        scratch_shapes=[pltpu.VMEM((2, tm * TOKEN_SUBLANES, LANES), F32), pltpu.SemaphoreType.DMA((2,))],
    )
    outs = pl.pallas_call(
        functools.partial(_inproj_combine_kernel, **dims),
        grid_spec=grid_spec,
        out_shape=[jax.ShapeDtypeStruct((b, lt, d), F32)] + out_shape,
        input_output_aliases={6: 0},
        compiler_params=_cparams(("arbitrary", "arbitrary")),
        name="combine_in_projection",
    )(pstart, cls_blk, rank_blk, cls_blk, rank_blk, ys, xx, mods_prev, mods, g.reshape(1, d), w_all,
      cos_t, sa_t, sb_t)
    return tuple(outs)


def rope_tables(lc, s):
    rows = s // GRID_W
    row = jnp.repeat(jnp.arange(rows), GRID_W).astype(F32)
    col = jnp.tile(jnp.arange(GRID_W), rows).astype(F32)
    inv = ROPE_BASE ** (-jnp.arange(ROPE_AXIS_FREQS, dtype=F32) / ROPE_AXIS_FREQS)
    ang_r = row[:, None] * inv
    ang_c = col[:, None] * inv
    ang = jnp.concatenate([ang_r, ang_r, ang_c, ang_c], axis=-1)
    cos = jnp.concatenate([jnp.ones((lc, QK_HEAD_DIM), F32), jnp.cos(ang)], axis=0)
    sin = jnp.concatenate([jnp.zeros((lc, QK_HEAD_DIM), F32), jnp.sin(ang)], axis=0)
    first_half = (jnp.arange(QK_HEAD_DIM) % (2 * ROPE_AXIS_FREQS)) < ROPE_AXIS_FREQS
    sa = jnp.where(first_half, -sin, 0.0)
    sb = jnp.where(first_half, 0.0, sin)
    reps = N_HEADS * 2
    return jnp.tile(cos, (1, reps)), jnp.tile(sa, (1, reps)), jnp.tile(sb, (1, reps))


def _cmul(ar, ai, br, bi):
    return ar * br - ai * bi, ar * bi + ai * br


def s5_chunk_weights(lam_re, lam_im, log_dt, b_re, b_im, c_re, c_im, d_skip):
    c = SSM_CHUNK
    lr = lam_re.astype(F32)
    li = lam_im.astype(F32)
    dt = jnp.exp(log_dt.astype(F32))[..., None]
    decay = jnp.exp(lr * dt)
    ab_re = decay * jnp.cos(li * dt)
    ab_im = decay * jnp.sin(li * dt)
    den = lr * lr + li * li
    nr = ab_re - 1.0
    ni = ab_im
    cr = ((nr * lr + ni * li) / den)[..., None]
    ci = ((ni * lr - nr * li) / den)[..., None]
    br = b_re.astype(F32)
    bi = b_im.astype(F32)
    bb_re = cr * br - ci * bi
    bb_im = cr * bi + ci * br
    ccr = c_re.astype(F32)
    cci = c_im.astype(F32)

    pr = [jnp.ones_like(ab_re)]
    pi = [jnp.zeros_like(ab_im)]
    for _ in range(c):
        r, i = _cmul(pr[-1], pi[-1], ab_re, ab_im)
        pr.append(r)
        pi.append(i)
    pw_re = jnp.stack(pr)
    pw_im = jnp.stack(pi)

    pb_re, pb_im = _cmul(pw_re[..., None], pw_im[..., None], bb_re[None], bb_im[None])
    cp_re, cp_im = _cmul(ccr[None], cci[None], pw_re[:, :, :, None, :], pw_im[:, :, :, None, :])
    kk = (jnp.einsum('dgop,tdgpi->tdgoi', ccr, pb_re[:c], precision=HIGHEST)
          - jnp.einsum('dgop,tdgpi->tdgoi', cci, pb_im[:c], precision=HIGHEST))

    g = lr.shape[1]
    h = SSM_GROUP
    p = SSM_STATE
    ii = jnp.arange(c)[:, None]
    jj = jnp.arange(c)[None, :]
    lags = jnp.arange(c)[:, None, None]
    shift_f = ((jj - ii)[None] == lags).astype(F32)
    shift_b = ((ii - jj)[None] == lags).astype(F32)
    tt = (jnp.einsum('tij,tgon->ginjo', shift_f, kk[:, 0], precision=HIGHEST)
          + jnp.einsum('tij,tgon->ginjo', shift_b, kk[:, 1], precision=HIGHEST))
    dd = d_skip.astype(F32).reshape(g, h)
    eye_t = jnp.eye(c, dtype=F32)
    eye_h = jnp.eye(h, dtype=F32)
    tt = tt + (eye_t[None, :, None, :, None] * eye_h[None, None, :, None, :]
               * dd[:, None, :, None, None])
    tt = tt.reshape(g, c * h, c * h)

    def s_cols(pbx, d, rev):
        m = pbx[:c, d]
        if rev:
            m = m[::-1]
        return jnp.transpose(m, (1, 0, 3, 2)).reshape(g, c * h, p)

    w1 = jnp.concatenate([
        tt,
        s_cols(pb_re, 0, True), s_cols(pb_re, 1, False),
        s_cols(pb_im, 0, True), s_cols(pb_im, 1, False)], axis=-1)

    def o_rows(cpx, d, sign, fwd):
        m = cpx[1:c + 1, d]
        if not fwd:
            m = m[::-1]
        return sign * jnp.transpose(m, (1, 3, 0, 2)).reshape(g, p, c * h)

    w2 = jnp.concatenate([
        o_rows(cp_re, 0, 1.0, True), o_rows(cp_re, 1, 1.0, False),
        o_rows(cp_im, 0, -1.0, True), o_rows(cp_im, 1, -1.0, False)], axis=1)

    a16 = jnp.zeros((g, 8, 2 * p), F32)
    a16 = a16.at[:, 0, :].set(jnp.concatenate([pw_re[c, 0], pw_re[c, 1]], axis=-1))
    a16 = a16.at[:, 1, :].set(jnp.concatenate([pw_im[c, 0], pw_im[c, 1]], axis=-1))
    return w1.astype(BF16), w2.astype(BF16), a16


def _s5_kernel(x_ref, w1_ref, w2_ref, a_ref, y_ref, z_ref, hin_ref, *, nb, nc_ctx, nc_lat):
    ch = x_ref.shape[-1]
    p = SSM_STATE
    z_ref[...] = _dot(x_ref[0], w1_ref[0])
    a_re = a_ref[0, 0:1, :]
    a_im = a_ref[0, 1:2, :]
    fwd_lane = lax.broadcasted_iota(jnp.int32, (nb, 2 * p), 1) < p

    def phase(base, n, carry):
        def step(i, hc):
            h_re, h_im = hc
            rf = pl.multiple_of((base + i) * nb, nb)
            rb = pl.multiple_of((base + n - 1 - i) * nb, nb)
            hin_ref[pl.ds(rf, nb), 0:p] = h_re[:, 0:p]
            hin_ref[pl.ds(rb, nb), p:2 * p] = h_re[:, p:2 * p]
            hin_ref[pl.ds(rf, nb), 2 * p:3 * p] = h_im[:, 0:p]
            hin_ref[pl.ds(rb, nb), 3 * p:4 * p] = h_im[:, p:2 * p]
            s_re = jnp.where(fwd_lane, z_ref[pl.ds(rf, nb), ch:ch + 2 * p],
                             z_ref[pl.ds(rb, nb), ch:ch + 2 * p])
            s_im = jnp.where(fwd_lane, z_ref[pl.ds(rf, nb), ch + 2 * p:ch + 4 * p],
                             z_ref[pl.ds(rb, nb), ch + 2 * p:ch + 4 * p])
            n_re = a_re * h_re - a_im * h_im + s_re
            n_im = a_re * h_im + a_im * h_re + s_im
            return n_re, n_im
        return lax.fori_loop(0, n, step, carry)

    zero = jnp.zeros((nb, 2 * p), F32)
    hc = phase(0, nc_ctx, (zero, zero))
    phase(nc_ctx, nc_lat, hc)
    y = z_ref[:, 0:ch] + _dot(hin_ref[...].astype(BF16), w2_ref[0])
    y_ref[0] = jax.nn.gelu(y).astype(BF16)


def _regroup_kernel(x_ref, o_ref, *, n_groups):
    h = SSM_GROUP
    for t in range(SSM_CHUNK):
        for g in range(n_groups):
            src = (t * n_groups + g) * h
            o_ref[g, :, t * h:(t + 1) * h] = x_ref[:, src:src + h]


def _ungroup_kernel(y_ref, o_ref, *, n_groups):
    h = SSM_GROUP
    for t in range(SSM_CHUNK):
        for g in range(n_groups):
            dst = (t * n_groups + g) * h
            o_ref[:, dst:dst + h] = y_ref[g, :, t * h:(t + 1) * h]


REGROUP_ROWS = 128


def _regroup(x2, n_groups, inverse):
    c = SSM_CHUNK
    h = SSM_GROUP
    if inverse:
        g, m, _ = x2.shape
    else:
        m = x2.shape[0]
        g = n_groups
    rb = min(REGROUP_ROWS, m)
    assert m % rb == 0
    wide = pl.BlockSpec((rb, c * g * h), lambda i: (i, 0))
    grouped = pl.BlockSpec((g, rb, c * h), lambda i: (0, i, 0))
    kern = functools.partial(_ungroup_kernel if inverse else _regroup_kernel, n_groups=g)
    return pl.pallas_call(
        kern,
        grid=(m // rb,),
        in_specs=[grouped if inverse else wide],
        out_specs=wide if inverse else grouped,
        out_shape=jax.ShapeDtypeStruct((m, c * g * h) if inverse else (g, m, c * h), x2.dtype),
        compiler_params=_cparams(("arbitrary",)),
        name="s5_ungroup" if inverse else "s5_regroup",
    )(x2)


def s5_mixer(u4, w1, w2, a16, layer, lc):
    nc, b, c, dssm = u4.shape
    h = SSM_GROUP
    g = dssm // h
    lt = nc * c
    m = nc * b
    ug = _regroup(u4.reshape(m, c * dssm), g, False)
    kern = functools.partial(_s5_kernel, nb=b, nc_ctx=lc // c, nc_lat=(lt - lc) // c)
    yg = pl.pallas_call(
        kern,
        grid=(g,),
        in_specs=[
            pl.BlockSpec((1, m, c * h), lambda i: (i, 0, 0)),
            pl.BlockSpec((None, 1) + w1.shape[2:], lambda i: (layer, i, 0, 0)),
            pl.BlockSpec((None, 1) + w2.shape[2:], lambda i: (layer, i, 0, 0)),
            pl.BlockSpec((None, 1, 8, 2 * SSM_STATE), lambda i: (layer, i, 0, 0)),
        ],
        out_specs=pl.BlockSpec((1, m, c * h), lambda i: (i, 0, 0)),
        out_shape=jax.ShapeDtypeStruct((g, m, c * h), BF16),
        scratch_shapes=[pltpu.VMEM((m, w1.shape[-1]), F32), pltpu.VMEM((m, 4 * SSM_STATE), F32)],
        compiler_params=_cparams(("arbitrary",)),
        name="s5_mixer",
    )(ug, w1, w2, a16)
    return _regroup(yg, g, True).reshape(nc, b, c, dssm)


def _attn_kernel(lam_ref, q_ref, k_ref, v_ref, g_ref, o_ref, *, n_ctx_tiles, lc, out_scale):
    j = pl.program_id(2)
    lam = lam_ref[0]
    q = q_ref[0]
    lo = lax.broadcasted_iota(jnp.int32, q.shape, 1) < QK_HEAD_DIM
    zq = jnp.zeros_like(q)
    q1 = jnp.where(lo, q, zq)
    q2 = jnp.where(lo, zq, q)
    nt = (((1,), (1,)), ((), ()))

    def run(nk):
        k = k_ref[0, 0:nk, :]
        v = v_ref[0, 0:nk, :]
        s1 = lax.dot_general(q1, k, nt, preferred_element_type=F32)
        s2 = lax.dot_general(q2, k, nt, preferred_element_type=F32)
        e1 = jnp.exp2(s1 - jnp.max(s1, axis=-1, keepdims=True))
        e2 = jnp.exp2(s2 - jnp.max(s2, axis=-1, keepdims=True))
        l1 = jnp.sum(e1, axis=-1, keepdims=True)
        l2 = jnp.sum(e2, axis=-1, keepdims=True)
        a = e1 - e2 * (lam * l1 / l2)
        o = _dot(a.astype(BF16), v) * (1.0 / l1)
        ms = jnp.mean(o * o, axis=-1, keepdims=True)
        o = (o * lax.rsqrt(ms + SUBLN_EPS)) * g_ref[...] * out_scale
        o_ref[0] = o.astype(BF16)

    @pl.when(j < n_ctx_tiles)
    def _():
        run(lc)

    @pl.when(j >= n_ctx_tiles)
    def _():
        run(k_ref.shape[1])


def diff_attention(q, k, v, lam, subln_g, lc, lam_init):
    b, lt, _ = q.shape
    tm = TOKEN_TILE
    nj = lt // tm
    hd = 2 * QK_HEAD_DIM
    kern = functools.partial(_attn_kernel, n_ctx_tiles=lc // tm, lc=lc, out_scale=1.0 - lam_init)
    return pl.pallas_call(
        kern,
        grid=(b, N_HEADS, nj),
        in_specs=[
            pl.BlockSpec(memory_space=pltpu.SMEM),
            pl.BlockSpec((1, tm, hd), lambda i, h, j: (i, j, h)),
            pl.BlockSpec((1, lt, hd), lambda i, h, j: (i, 0, h)),
            pl.BlockSpec((1, lt, V_HEAD_DIM), lambda i, h, j: (i, 0, h)),
            pl.BlockSpec((1, V_HEAD_DIM), lambda i, h, j: (0, 0)),
        ],
        out_specs=pl.BlockSpec((1, tm, V_HEAD_DIM), lambda i, h, j: (i, j, h)),
        out_shape=jax.ShapeDtypeStruct((b, lt, N_HEADS * V_HEAD_DIM), BF16),
        compiler_params=_cparams(("arbitrary", "arbitrary", "arbitrary")),
        name="diff_attention",
    )(lam.reshape(1), q, k, v, subln_g.reshape(1, V_HEAD_DIM))


def _route(logits_t, rb):
    aff = jax.nn.sigmoid(logits_t)
    sel = aff + rb
    s = [sel[e:e + 1] for e in range(N_EXPERTS)]
    npg = EXPERTS_PER_GROUP
    gscore = []
    for g in range(N_EXPERT_GROUPS):
        best = None
        for (i, j) in PAIRS:
            ps = s[g * npg + i] + s[g * npg + j]
            best = ps if best is None else jnp.maximum(best, ps)
        gscore.append(best)
    g_sel = jnp.zeros_like(gscore[0], dtype=jnp.int32)
    g_best = gscore[0]
    for g in range(1, N_EXPERT_GROUPS):
        take = gscore[g] > g_best
        g_sel = jnp.where(take, g, g_sel)
        g_best = jnp.where(take, gscore[g], g_best)

    def pick(rows, i):
        out = rows[i]
        for g in range(1, N_EXPERT_GROUPS):
            out = jnp.where(g_sel == g, rows[g * npg + i], out)
        return out

    xs = [pick(s, i) for i in range(npg)]
    i1 = jnp.zeros_like(g_sel)
    v1 = xs[0]
    for i in range(1, npg):
        take = xs[i] > v1
        i1 = jnp.where(take, i, i1)
        v1 = jnp.where(take, xs[i], v1)
    i2 = jnp.full_like(g_sel, -1)
    v2 = jnp.full_like(v1, -jnp.inf)
    for i in range(npg):
        take = (i1 != i) & ((i2 < 0) | (xs[i] > v2))
        i2 = jnp.where(take, i, i2)
        v2 = jnp.where(take, xs[i], v2)
    lo = jnp.minimum(i1, i2)
    hi = jnp.maximum(i1, i2)
    pair = jnp.zeros_like(lo)
    for pi_, (i, j) in enumerate(PAIRS):
        pair = jnp.where((lo == i) & (hi == j), pi_, pair)
    return g_sel * len(PAIRS) + pair


def _merge_kernel(x_ref, yg_ref, o_ref, gate_ref, mod_ref, g2_ref, wglu_ref, wat_ref, wo_ref,
                  rw_ref, rb_ref, xo_ref, h2_ref, route_ref, cnt_ref):
    d = x_ref.shape[-1]
    tm = x_ref.shape[1]
    m = mod_ref[0]
    yg = jnp.concatenate([yg_ref[ci, 0] for ci in range(yg_ref.shape[0])], axis=0)
    ab = _dot(yg, wglu_ref[...])
    p_s = ab[:, 0:d] * jax.nn.sigmoid(ab[:, d:2 * d])
    p_a = _dot(o_ref[0], wat_ref[...])
    gate = gate_ref[0]
    mix = gate[:, 0:d].astype(F32) * p_s + gate[:, d:2 * d].astype(F32) * p_a
    x = x_ref[0] + m[2:3] * _dot(mix.astype(BF16), wo_ref[...])
    xo_ref[0] = x
    ms = jnp.mean(x * x, axis=-1, keepdims=True)
    y = x * lax.rsqrt(ms + NORM_EPS)
    h2 = (y * g2_ref[...]) * (1.0 + m[4:5]) + m[3:4]
    lanes = 128
    for kk in range(d // lanes):
        h2_ref[pl.ds(kk, tm, stride=d // lanes), :] = h2[:, kk * lanes:(kk + 1) * lanes]
    lt = _router_logits(h2, rw_ref).T[0:N_EXPERTS]
    cls = _route(lt, rb_ref[...])

    @pl.when((pl.program_id(0) == 0) & (pl.program_id(1) == 0))
    def _():
        cnt_ref[...] = jnp.zeros_like(cnt_ref)

    member = lax.broadcasted_iota(jnp.int32, (cnt_ref.shape[0], tm), 0) == cls
    earlier = (lax.broadcasted_iota(jnp.int32, (tm, tm), 0) < lax.broadcasted_iota(jnp.int32, (tm, tm), 1))
    before = _dot(member.astype(BF16), earlier.astype(BF16))
    rank = jnp.sum(jnp.where(member, before + cnt_ref[:, 0:1], 0.0), axis=0, keepdims=True)
    cnt_ref[...] = cnt_ref[...] + jnp.sum(member.astype(F32), axis=1, keepdims=True)
    route_ref[...] = jnp.concatenate([rank.astype(jnp.int32), cls, jnp.zeros((6, tm), jnp.int32)], axis=0)


def merge_and_route(xx, yg, o, gate, mods, g2, wglu, wat, wo, layer, rw_pad, rb, n_ctx_tiles, latent_only=False):
    b, lt, d = xx.shape
    tm = TOKEN_TILE
    off = n_ctx_tiles if latent_only else 0
    nj = lt // tm - off
    t = b * nj * tm
    tok = lambda i, j: (i, j + off, 0)
    const = lambda i, j: (0, 0)

    def mod_idx(i, j):
        return (jnp.where(j + off < n_ctx_tiles, b, i), 0, 0)

    return pl.pallas_call(
        _merge_kernel,
        grid=(b, nj),
        in_specs=[
            pl.BlockSpec((1, tm, d), tok),
            pl.BlockSpec((tm // SSM_CHUNK, 1, SSM_CHUNK, yg.shape[-1]), lambda i, j: (j + off, i, 0, 0)),
            pl.BlockSpec((1, tm, o.shape[-1]), tok),
            pl.BlockSpec((1, tm, gate.shape[-1]), tok),
            pl.BlockSpec((1, 6, d), mod_idx),
            pl.BlockSpec((1, d), const),
            pl.BlockSpec((None,) + wglu.shape[1:], lambda i, j: (layer, 0, 0)),
            pl.BlockSpec((None,) + wat.shape[1:], lambda i, j: (layer, 0, 0)),
            pl.BlockSpec((None,) + wo.shape[1:], lambda i, j: (layer, 0, 0)),
            pl.BlockSpec(rw_pad.shape, const),
            pl.BlockSpec(rb.shape, const),
        ],
        out_specs=[
            pl.BlockSpec((1, tm, d), tok),
            pl.BlockSpec((tm * (d // 128), 128), lambda i, j: (i * nj + j, 0)),
            pl.BlockSpec((8, tm), lambda i, j: (0, i * nj + j)),
            pl.BlockSpec((LANES, LANES), const),
        ],
        out_shape=[
            jax.ShapeDtypeStruct((b, lt, d), F32),
            jax.ShapeDtypeStruct((t * (d // 128), 128), F32),
            jax.ShapeDtypeStruct((8, t), jnp.int32),
            jax.ShapeDtypeStruct((LANES, LANES), F32),
        ],
        input_output_aliases={0: 0},
        compiler_params=_cparams(("arbitrary", "arbitrary")),
        name="merge_and_route",
    )(xx, yg, o, gate, mods, g2.reshape(1, d), wglu, wat, wo, rw_pad, rb)


def block_tables(counts, t):
    r = EXPERT_BLOCK
    ncls = N_CLASSES
    cap = -(-(t + ncls * (r - 1)) // r) * r
    nblk = cap // r
    padded = (counts + r - 1) // r * r
    pend = jnp.cumsum(padded)
    pstart = pend - padded
    n_valid = pend[-1] // r
    blk = jnp.arange(nblk, dtype=jnp.int32)
    blk_cls = jnp.minimum(jnp.sum((pend[None, :] <= (blk * r)[:, None]).astype(jnp.int32), axis=1), ncls - 1)
    last_cls = jnp.max(jnp.where(counts > 0, jnp.arange(ncls, dtype=jnp.int32), 0))
    blk_cls = jnp.where(blk < n_valid, blk_cls, last_cls)
    pair_lo = jnp.array([p[0] for p in PAIRS], jnp.int32)
    pair_hi = jnp.array([p[1] for p in PAIRS], jnp.int32)
    grp = blk_cls // len(PAIRS)
    pr = blk_cls % len(PAIRS)
    e_lo = grp * EXPERTS_PER_GROUP + pair_lo[pr]
    e_hi = grp * EXPERTS_PER_GROUP + pair_hi[pr]
    pstart_pad = jnp.pad(pstart.astype(jnp.int32), (0, 32 - ncls))
    return pstart_pad, e_lo, e_hi, n_valid.astype(jnp.int32).reshape(1), cap


TOKEN_SUBLANES = 8
LANES = 128
GATHER_UNROLL = 8


def _slot_copies(pstart_ref, cls_ref, rank_ref, n, make_copy, sem):
    assert n % GATHER_UNROLL == 0

    def issue(i, carry):
        for uu in range(GATHER_UNROLL):
            r = i * GATHER_UNROLL + uu
            slot = pstart_ref[cls_ref[0, 0, r]] + rank_ref[0, 0, r]
            make_copy(r, slot, sem).start(priority=uu % 2)
        return carry
    lax.fori_loop(0, n // GATHER_UNROLL, issue, 0)


def _tile_rows(ref, first, n):
    ts = TOKEN_SUBLANES
    return ref.at[pl.ds(pl.multiple_of(first * ts, ts), n * ts)]


def _tiles_to_rows(ref, n):
    ts = TOKEN_SUBLANES
    return jnp.concatenate([ref[pl.ds(kk, n, stride=ts), :] for kk in range(ts)], axis=1)


def _dispatch_kernel(pstart_ref, cls_ref, rank_ref, src_ref, zero_ref, dst_ref, sem):
    del zero_ref
    n = cls_ref.shape[-1]

    def make_copy(r, slot, s):
        return pltpu.make_async_copy(_tile_rows(src_ref, r, 1), _tile_rows(dst_ref, slot, 1), s)

    _slot_copies(pstart_ref, cls_ref, rank_ref, n, make_copy, sem)
    pltpu.make_async_copy(src_ref, _tile_rows(dst_ref, 0, n), sem).wait()


def dispatch_tokens(h2, pstart, cls_blk, rank_blk, cap, init):
    ts = TOKEN_SUBLANES
    nblk, _, n = cls_blk.shape
    blk = lambda i, ps: (i, 0, 0)
    grid_spec = pltpu.PrefetchScalarGridSpec(
        num_scalar_prefetch=1,
        grid=(nblk,),
        in_specs=[
            pl.BlockSpec((1, 1, n), blk, memory_space=pltpu.SMEM),
            pl.BlockSpec((1, 1, n), blk, memory_space=pltpu.SMEM),
            pl.BlockSpec((n * ts, LANES), lambda i, ps: (i, 0)),
            pl.BlockSpec(memory_space=pl.ANY),
        ],
        out_specs=pl.BlockSpec(memory_space=pl.ANY),
        scratch_shapes=[pltpu.SemaphoreType.DMA(())],
    )
    return pl.pallas_call(
        _dispatch_kernel,
        grid_spec=grid_spec,
        out_shape=jax.ShapeDtypeStruct((cap * ts, LANES), h2.dtype),
        input_output_aliases={4: 0},
        compiler_params=_cparams(("arbitrary",)),
        name="dispatch_tokens",
    )(pstart, cls_blk, rank_blk, h2, init)


def _combine_kernel(pstart_ref, cls_ref, rank_ref, tab_ref, x_ref, mod_ref, gf_ref, o_ref, buf_ref, sem, *, final):
    n = x_ref.shape[1]

    def make_copy(r, slot, s):
        return pltpu.make_async_copy(_tile_rows(tab_ref, slot, 1), _tile_rows(buf_ref, r, 1), s)

    _slot_copies(pstart_ref, cls_ref, rank_ref, n, make_copy, sem)
    pltpu.make_async_copy(_tile_rows(tab_ref, 0, n), buf_ref, sem).wait()
    g2 = mod_ref[0][5:6]
    x = x_ref[0] + g2 * _tiles_to_rows(buf_ref, n)
    if final:
        ms = jnp.mean(x * x, axis=-1, keepdims=True)
        x = (x * lax.rsqrt(ms + NORM_EPS)) * gf_ref[...]
    o_ref[0] = x


def combine_residual(xx, ys, pstart, cls_blk, rank_blk, mods, n_ctx_tiles, final_g=None):
    b, lt, d = xx.shape
    tm = TOKEN_TILE
    nj = lt // tm
    final = final_g is not None
    off = n_ctx_tiles if final else 0
    tok = lambda i, j, ps: (i, j + off, 0)
    blk = lambda i, j, ps: (i * (nj - off) + j, 0, 0)

    def mod_idx(i, j, ps):
        return (jnp.where(j + off < n_ctx_tiles, b, i), 0, 0)

    grid_spec = pltpu.PrefetchScalarGridSpec(
        num_scalar_prefetch=1,
        grid=(b, nj - off),
        in_specs=[
            pl.BlockSpec((1, 1, tm), blk, memory_space=pltpu.SMEM),
            pl.BlockSpec((1, 1, tm), blk, memory_space=pltpu.SMEM),
            pl.BlockSpec(memory_space=pl.ANY),
            pl.BlockSpec((1, tm, d), tok),
            pl.BlockSpec((1, 6, d), mod_idx),
            pl.BlockSpec((1, d), lambda i, j, ps: (0, 0)),
        ],
        out_specs=pl.BlockSpec((1, tm, d), lambda i, j, ps: (i, j, 0)),
        scratch_shapes=[pltpu.VMEM((tm * TOKEN_SUBLANES, LANES), F32), pltpu.SemaphoreType.DMA(())],
    )
    gf = (final_g if final else jnp.ones((d,), F32)).astype(F32).reshape(1, d)
    return pl.pallas_call(
        functools.partial(_combine_kernel, final=final),
        grid_spec=grid_spec,
        out_shape=jax.ShapeDtypeStruct((b, lt - off * tm, d), F32),
        input_output_aliases={} if final else {4: 0},
        compiler_params=_cparams(("arbitrary", "arbitrary")),
        name="combine_final" if final else "combine_residual",
    )(pstart, cls_blk, rank_blk, ys, xx, mods, gf)


def _router_logits(h2, rw_ref):
    n = h2.shape[0]
    hi = h2.astype(BF16)
    lo = (h2 - hi.astype(F32)).astype(BF16)
    pp = _dot(jnp.concatenate([hi, lo], axis=0), rw_ref[...])
    return (pp[0:n, 0:LANES] + pp[0:n, LANES:2 * LANES]) + (pp[n:2 * n, 0:LANES] + pp[n:2 * n, LANES:2 * LANES])


def _expert_kernel(elo_ref, ehi_ref, nv_ref, x_ref, rw_ref, w1a, w3a, w2a, w1b, w3b, w2b, o_ref):
    i = pl.program_id(0)
    n = x_ref.shape[0] // TOKEN_SUBLANES

    @pl.when(i < nv_ref[0])
    def _():
        x = _tiles_to_rows(x_ref, n)
        xb = x.astype(BF16)

        def ffn(w1, w3, w2):
            h1 = _dot(xb, w1[...])
            h3 = _dot(xb, w3[...])
            hid = (h1 * jax.nn.sigmoid(h1)) * h3
            return _dot(hid.astype(BF16), w2[...])

        aff = jax.nn.sigmoid(_router_logits(x, rw_ref))
        lane = lax.broadcasted_iota(jnp.int32, aff.shape, 1)
        a_lo = jnp.sum(jnp.where(lane == elo_ref[i], aff, 0.0), axis=1, keepdims=True)
        a_hi = jnp.sum(jnp.where(lane == ehi_ref[i], aff, 0.0), axis=1, keepdims=True)
        tot = a_lo + a_hi
        y = (a_lo / tot) * ffn(w1a, w3a, w2a) + (a_hi / tot) * ffn(w1b, w3b, w2b)
        for kk in range(TOKEN_SUBLANES):
            o_ref[pl.ds(kk, n, stride=TOKEN_SUBLANES), :] = y[:, kk * LANES:(kk + 1) * LANES]

    @pl.when(i >= nv_ref[0])
    def _():
        o_ref[...] = jnp.zeros_like(o_ref)


def expert_blocks(xs, rw_pad, e_lo, e_hi, n_valid, w1, w3, w2, layer):
    ts = TOKEN_SUBLANES
    cap = xs.shape[0] // ts
    d = ts * LANES
    r = EXPERT_BLOCK
    nblk = cap // r
    de = w1.shape[-1]
    row = lambda i, elo, ehi, nv: (i, 0)
    wlo = lambda i, elo, ehi, nv: (layer, elo[i], 0, 0)
    whi = lambda i, elo, ehi, nv: (layer, ehi[i], 0, 0)
    grid_spec = pltpu.PrefetchScalarGridSpec(
        num_scalar_prefetch=3,
        grid=(nblk,),
        in_specs=[
            pl.BlockSpec((r * ts, LANES), row),
            pl.BlockSpec(rw_pad.shape, lambda i, elo, ehi, nv: (0, 0)),
            pl.BlockSpec((None, None, d, de), wlo),
            pl.BlockSpec((None, None, d, de), wlo),
            pl.BlockSpec((None, None, de, d), wlo),
            pl.BlockSpec((None, None, d, de), whi),
            pl.BlockSpec((None, None, d, de), whi),
            pl.BlockSpec((None, None, de, d), whi),
        ],
        out_specs=pl.BlockSpec((r * ts, LANES), row),
    )
    return pl.pallas_call(
        _expert_kernel,
        grid_spec=grid_spec,
        out_shape=jax.ShapeDtypeStruct((cap * ts, LANES), F32),
        compiler_params=_cparams(("arbitrary",)),
        name="expert_blocks",
    )(e_lo, e_hi, n_valid, xs, rw_pad, w1, w3, w2, w1, w3, w2)


def kernel(x, c, ctx, c_ctx, ada_w, ada_b, norm1_g, norm2_g, final_g, w_in,
           ssm_lam_re, ssm_lam_im, ssm_log_dt, ssm_b_re, ssm_b_im, ssm_c_re, ssm_c_im, ssm_d, w_glu,
           attn_lambda, attn_subln_g, w_attn_out, w_o, router_w, router_b, exp_w1, exp_w3, exp_w2):
    b, s, d = x.shape
    lc = ctx.shape[1]
    depth = w_in.shape[0]
    tm = TOKEN_TILE
    assert lc % tm == 0 and s % tm == 0 and s % GRID_W == 0 and lc % SSM_CHUNK == 0
    n_ctx_tiles = lc // tm
    lt = lc + s

    xx = jnp.concatenate([ctx, x], axis=1)

    n_mod_rows = -(-(b + 1) // 8) * 8
    c_all = jnp.zeros((n_mod_rows, d), F32).at[:b].set(c).at[b].set(c_ctx)
    mods_all = ada_modulation(c_all, ada_w, ada_b).reshape(depth, n_mod_rows, 6, d)

    cos_t, sa_t, sb_t = rope_tables(lc, s)
    assert d == TOKEN_SUBLANES * LANES
    rw_f = jnp.zeros((d, LANES), F32).at[:, :N_EXPERTS].set(router_w.astype(F32))
    rw_hi = rw_f.astype(BF16)
    rw_lo = (rw_f - rw_hi.astype(F32)).astype(BF16)
    rw_pad = jnp.concatenate([rw_hi, rw_lo], axis=1)
    rb = router_b.astype(F32).reshape(N_EXPERTS, 1)

    w_in_bf = w_in.astype(BF16)
    w_glu_bf = w_glu.astype(BF16)
    w_at_bf = w_attn_out.astype(BF16)
    w_o_bf = w_o.astype(BF16)
    e1_bf = exp_w1.astype(BF16)
    e3_bf = exp_w3.astype(BF16)
    e2_bf = exp_w2.astype(BF16)

    w1s, w2s, a16 = jax.vmap(s5_chunk_weights)(ssm_lam_re, ssm_lam_im, ssm_log_dt, ssm_b_re, ssm_b_im,
                                               ssm_c_re, ssm_c_im, ssm_d)

    pending = None
    xs = None
    for l in range(depth):
        lam_init = 0.8 - 0.6 * math.exp(-0.3 * l)
        mods = mods_all[l]
        xx, u, q, k, v, gate = in_projection(xx, mods, norm1_g[l], w_in_bf, l,
                                             cos_t, sa_t, sb_t, n_ctx_tiles, pending)

        yg = s5_mixer(u, w1s, w2s, a16, l, lc)

        lf = attn_lambda[l].astype(F32)
        lam = jnp.exp(jnp.sum(lf[0] * lf[1])) - jnp.exp(jnp.sum(lf[2] * lf[3])) + lam_init
        o = diff_attention(q, k, v, lam, attn_subln_g[l].astype(F32), lc, lam_init)

        xx, h2, ranks, hist = merge_and_route(xx, yg, o, gate, mods, norm2_g[l], w_glu_bf, w_at_bf, w_o_bf, l,
                                              rw_pad, rb, n_ctx_tiles, latent_only=l == depth - 1)

        n_tiles = ranks.shape[1] // tm
        rank_blk = ranks[0].reshape(n_tiles, 1, tm)
        cls_blk = ranks[1].reshape(n_tiles, 1, tm)
        counts = hist[:N_CLASSES, 0].astype(jnp.int32)
        pstart, e_lo, e_hi, n_valid, cap = block_tables(counts, b * lt)
        init = jnp.zeros((cap * TOKEN_SUBLANES, LANES), F32) if xs is None else xs
        xs = dispatch_tokens(h2, pstart, cls_blk, rank_blk, cap, init)
        ys = expert_blocks(xs, rw_pad, e_lo, e_hi, n_valid, e1_bf, e3_bf, e2_bf, l)
        pending = (ys, pstart, cls_blk, rank_blk, mods)

    ys, pstart, cls_blk, rank_blk, mods = pending
    return combine_residual(xx, ys, pstart, cls_blk, rank_blk, mods, n_ctx_tiles, final_g=final_g)
```

```python
import functools
import math

import jax
import jax.numpy as jnp
from jax import lax
from jax.experimental import pallas as pl
from jax.experimental.pallas import tpu as pltpu

F32 = jnp.float32
BF16 = jnp.bfloat16
HIGHEST = lax.Precision.HIGHEST

GRID_W = 64
SSM_GROUP = 16
SSM_STATE = 64
N_HEADS = 4
QK_HEAD_DIM = 64
V_HEAD_DIM = 128
ROPE_BASE = 10000.0
ROPE_AXIS_FREQS = QK_HEAD_DIM // 4
N_EXPERTS = 16
EXPERTS_PER_GROUP = 4
N_EXPERT_GROUPS = N_EXPERTS // EXPERTS_PER_GROUP
NORM_EPS = 1e-6
SUBLN_EPS = 1e-5

PAIRS = ((0, 1), (0, 2), (0, 3), (1, 2), (1, 3), (2, 3))
N_CLASSES = N_EXPERT_GROUPS * len(PAIRS)

TOKEN_TILE = 256
SSM_CHUNK = 16
EXPERT_BLOCK = 256
VMEM_LIMIT = 56 * 1024 * 1024


def _dot(a, b):
    return jnp.dot(a, b, preferred_element_type=F32)


def _cparams(sem):
    return pltpu.CompilerParams(dimension_semantics=sem, vmem_limit_bytes=VMEM_LIMIT)


def _ada_kernel(c_ref, w_ref, b_ref, o_ref):
    c = c_ref[...]
    s = c * jax.nn.sigmoid(c)
    o_ref[0] = jnp.dot(s, w_ref[0], precision=HIGHEST, preferred_element_type=F32) + b_ref[0]


def ada_modulation(c_all, ada_w, ada_b):
    depth, d, n = ada_w.shape
    r = c_all.shape[0]
    tn = n // 4
    return pl.pallas_call(
        _ada_kernel,
        grid=(depth, n // tn),
        in_specs=[
            pl.BlockSpec((r, d), lambda l, j: (0, 0)),
            pl.BlockSpec((1, d, tn), lambda l, j: (l, 0, j)),
            pl.BlockSpec((1, 1, tn), lambda l, j: (l, 0, j)),
        ],
        out_specs=pl.BlockSpec((1, r, tn), lambda l, j: (l, 0, j)),
        out_shape=jax.ShapeDtypeStruct((depth, r, n), F32),
        compiler_params=_cparams(("arbitrary", "arbitrary")),
        name="ada_modulation",
    )(c_all, ada_w, ada_b.reshape(depth, 1, n))


def _inproj_first_kernel(c_ref, x_ref, mod_ref, g_ref, w_ref, cos_ref, sa_ref, sb_ref,
                         xo_ref, u_ref, q_ref, k_ref, v_ref, gate_ref, *, n_ctx_tiles, d_ssm, d_qk, d_v):
    x = jnp.where(pl.program_id(1) < n_ctx_tiles, c_ref[0], x_ref[0])
    xo_ref[0] = x
    _inproj_core(x, mod_ref[0], g_ref, w_ref, cos_ref, sa_ref, sb_ref,
                 u_ref, q_ref, k_ref, v_ref, gate_ref, d_ssm, d_qk, d_v)


def _inproj_kernel(x_ref, mod_ref, g_ref, w_ref, cos_ref, sa_ref, sb_ref,
                   u_ref, q_ref, k_ref, v_ref, gate_ref, *, d_ssm, d_qk, d_v):
    _inproj_core(x_ref[0], mod_ref[0], g_ref, w_ref, cos_ref, sa_ref, sb_ref,
                 u_ref, q_ref, k_ref, v_ref, gate_ref, d_ssm, d_qk, d_v)


def _inproj_combine_kernel(pstart_ref, cls_ref, rank_ref, clsn_ref, rankn_ref, ys_ref, x_ref, modp_ref, mod_ref,
                           g_ref, w_ref, cos_ref, sa_ref, sb_ref,
                           xo_ref, u_ref, q_ref, k_ref, v_ref, gate_ref, buf_ref, sem, *, d_ssm, d_qk, d_v):
    n = x_ref.shape[1]
    nj = pl.num_programs(1)
    s = pl.program_id(0) * nj + pl.program_id(1)
    total = pl.num_programs(0) * nj
    slot = lax.rem(s, 2)

    def start_tile(c_ref, r_ref, dst):
        def make_copy(r, tok_slot, sm):
            return pltpu.make_async_copy(_tile_rows(ys_ref, tok_slot, 1), _tile_rows(buf_ref.at[dst], r, 1), sm)
        _slot_copies(pstart_ref, c_ref, r_ref, n, make_copy, sem.at[dst])

    @pl.when(s == 0)
    def _():
        start_tile(cls_ref, rank_ref, 0)

    pltpu.make_async_copy(_tile_rows(ys_ref, 0, n), buf_ref.at[slot], sem.at[slot]).wait()

    @pl.when(s + 1 < total)
    def _():
        start_tile(clsn_ref, rankn_ref, 1 - slot)

    x = x_ref[0] + modp_ref[0][5:6] * _tiles_to_rows(buf_ref.at[slot], n)
    xo_ref[0] = x
    _inproj_core(x, mod_ref[0], g_ref, w_ref, cos_ref, sa_ref, sb_ref,
                 u_ref, q_ref, k_ref, v_ref, gate_ref, d_ssm, d_qk, d_v)


def _inproj_core(x, m, g_ref, w_ref, cos_ref, sa_ref, sb_ref, u_ref, q_ref, k_ref, v_ref, gate_ref,
                 d_ssm, d_qk, d_v):
    ms = jnp.mean(x * x, axis=-1, keepdims=True)
    y = x * lax.rsqrt(ms + NORM_EPS)
    h = (y * g_ref[...]) * (1.0 + m[1:2]) + m[0:1]
    hb = h.astype(BF16)

    cos = cos_ref[...]
    sa = sa_ref[...]
    sb = sb_ref[...]
    half = ROPE_AXIS_FREQS

    def rope(t):
        return t * cos + pltpu.roll(t, d_qk - half, 1) * sa + pltpu.roll(t, half, 1) * sb

    o0 = 0
    u = _dot(hb, w_ref[:, o0:o0 + d_ssm]).astype(BF16)
    c = SSM_CHUNK
    for ci in range(u.shape[0] // c):
        u_ref[ci, 0] = u[ci * c:(ci + 1) * c, :]
    o0 += d_ssm
    q_ref[0] = (rope(_dot(hb, w_ref[:, o0:o0 + d_qk])) * (QK_HEAD_DIM ** -0.5 * math.log2(math.e))).astype(BF16)
    o0 += d_qk
    k_ref[0] = rope(_dot(hb, w_ref[:, o0:o0 + d_qk])).astype(BF16)
    o0 += d_qk
    v_ref[0] = _dot(hb, w_ref[:, o0:o0 + d_v]).astype(BF16)
    o0 += d_v
    n_gate = gate_ref.shape[-1]
    gate_ref[0] = jax.nn.sigmoid(_dot(hb, w_ref[:, o0:o0 + n_gate])).astype(BF16)


def in_projection(xx, mods, g, w_all, layer, cos_t, sa_t, sb_t, n_ctx_tiles, pending=None):
    first = isinstance(xx, tuple)
    if first:
        ctx_in, x_in = xx
        b, s_len, d = x_in.shape
        lt = ctx_in.shape[1] + s_len
    else:
        b, lt, d = xx.shape
    nj = lt // TOKEN_TILE
    n_lat = b
    d_ssm = d // 2
    d_qk = N_HEADS * 2 * QK_HEAD_DIM
    d_v = N_HEADS * V_HEAD_DIM
    n_gate = 2 * d
    tm = TOKEN_TILE
    fused = pending is not None

    def mod_idx(i, j, *_):
        return (jnp.where(j < n_ctx_tiles, n_lat, i), 0, 0)

    tok = lambda i, j, *_: (i, j, 0)
    const2 = lambda i, j, *_: (0, 0)
    table = lambda i, j, *_: (j, 0)
    in_specs = [
        pl.BlockSpec((1, tm, d), tok),
        pl.BlockSpec((1, 6, d), mod_idx),
        pl.BlockSpec((1, d), const2),
        pl.BlockSpec((None,) + w_all.shape[1:], lambda i, j, *_: (layer, 0, 0)),
        pl.BlockSpec((tm, d_qk), table),
        pl.BlockSpec((tm, d_qk), table),
        pl.BlockSpec((tm, d_qk), table),
    ]
    out_specs = [
        pl.BlockSpec((tm // SSM_CHUNK, 1, SSM_CHUNK, d_ssm), lambda i, j, *_: (j, i, 0, 0)),
        pl.BlockSpec((1, tm, d_qk), tok),
        pl.BlockSpec((1, tm, d_qk), tok),
        pl.BlockSpec((1, tm, d_v), tok),
        pl.BlockSpec((1, tm, n_gate), tok),
    ]
    out_shape = [
        jax.ShapeDtypeStruct((lt // SSM_CHUNK, b, SSM_CHUNK, d_ssm), BF16),
        jax.ShapeDtypeStruct((b, lt, d_qk), BF16),
        jax.ShapeDtypeStruct((b, lt, d_qk), BF16),
        jax.ShapeDtypeStruct((b, lt, d_v), BF16),
        jax.ShapeDtypeStruct((b, lt, n_gate), BF16),
    ]
    dims = dict(d_ssm=d_ssm, d_qk=d_qk, d_v=d_v)
    if first:
        nc = n_ctx_tiles
        outs = pl.pallas_call(
            functools.partial(_inproj_first_kernel, n_ctx_tiles=nc, **dims),
            grid=(b, nj),
            in_specs=[pl.BlockSpec((1, tm, d), lambda i, j: (i, jnp.minimum(j, nc - 1), 0)),
                      pl.BlockSpec((1, tm, d), lambda i, j: (i, jnp.maximum(j - nc, 0), 0))] + in_specs[1:],
            out_specs=[pl.BlockSpec((1, tm, d), tok)] + out_specs,
            out_shape=[jax.ShapeDtypeStruct((b, lt, d), F32)] + out_shape,
            compiler_params=_cparams(("arbitrary", "arbitrary")),
            name="join_in_projection",
        )(ctx_in, x_in, mods, g.reshape(1, d), w_all, cos_t, sa_t, sb_t)
        return tuple(outs)
    if not fused:
        outs = pl.pallas_call(
            functools.partial(_inproj_kernel, **dims),
            grid=(b, nj),
            in_specs=in_specs,
            out_specs=out_specs,
            out_shape=out_shape,
            compiler_params=_cparams(("arbitrary", "arbitrary")),
            name="in_projection",
        )(xx, mods, g.reshape(1, d), w_all, cos_t, sa_t, sb_t)
        return (xx,) + tuple(outs)

    ys, pstart, cls_blk, rank_blk, mods_prev = pending
    last_tile = b * nj - 1
    this_blk = lambda i, j, ps: (i * nj + j, 0, 0)
    next_blk = lambda i, j, ps: (jnp.minimum(i * nj + j + 1, last_tile), 0, 0)
    smem = lambda f: pl.BlockSpec((1, 1, tm), f, memory_space=pltpu.SMEM)
    grid_spec = pltpu.PrefetchScalarGridSpec(
        num_scalar_prefetch=1,
        grid=(b, nj),
        in_specs=[smem(this_blk), smem(this_blk), smem(next_blk), smem(next_blk),
                  pl.BlockSpec(memory_space=pl.ANY),
                  in_specs[0], pl.BlockSpec((1, 6, d), mod_idx)] + in_specs[1:],
        out_specs=[pl.BlockSpec((1, tm, d), tok)] + out_specs,
        scratch_shapes=[pltpu.VMEM((2, tm * TOKEN_SUBLANES, LANES), F32), pltpu.SemaphoreType.DMA((2,))],
    )
    outs = pl.pallas_call(
        functools.partial(_inproj_combine_kernel, **dims),
        grid_spec=grid_spec,
        out_shape=[jax.ShapeDtypeStruct((b, lt, d), F32)] + out_shape,
        input_output_aliases={6: 0},
        compiler_params=_cparams(("arbitrary", "arbitrary")),
        name="combine_in_projection",
    )(pstart, cls_blk, rank_blk, cls_blk, rank_blk, ys, xx, mods_prev, mods, g.reshape(1, d), w_all,
      cos_t, sa_t, sb_t)
    return tuple(outs)


def rope_tables(lc, s):
    rows = s // GRID_W
    row = jnp.repeat(jnp.arange(rows), GRID_W).astype(F32)
    col = jnp.tile(jnp.arange(GRID_W), rows).astype(F32)
    inv = ROPE_BASE ** (-jnp.arange(ROPE_AXIS_FREQS, dtype=F32) / ROPE_AXIS_FREQS)
    ang_r = row[:, None] * inv
    ang_c = col[:, None] * inv
    ang = jnp.concatenate([ang_r, ang_r, ang_c, ang_c], axis=-1)
    cos = jnp.concatenate([jnp.ones((lc, QK_HEAD_DIM), F32), jnp.cos(ang)], axis=0)
    sin = jnp.concatenate([jnp.zeros((lc, QK_HEAD_DIM), F32), jnp.sin(ang)], axis=0)
    first_half = (jnp.arange(QK_HEAD_DIM) % (2 * ROPE_AXIS_FREQS)) < ROPE_AXIS_FREQS
    sa = jnp.where(first_half, -sin, 0.0)
    sb = jnp.where(first_half, 0.0, sin)
    reps = N_HEADS * 2
    return jnp.tile(cos, (1, reps)), jnp.tile(sa, (1, reps)), jnp.tile(sb, (1, reps))


def _cmul(ar, ai, br, bi):
    return ar * br - ai * bi, ar * bi + ai * br


def s5_chunk_weights(lam_re, lam_im, log_dt, b_re, b_im, c_re, c_im, d_skip):
    c = SSM_CHUNK
    lr = lam_re.astype(F32)
    li = lam_im.astype(F32)
    dt = jnp.exp(log_dt.astype(F32))[..., None]
    decay = jnp.exp(lr * dt)
    ab_re = decay * jnp.cos(li * dt)
    ab_im = decay * jnp.sin(li * dt)
    den = lr * lr + li * li
    nr = ab_re - 1.0
    ni = ab_im
    cr = ((nr * lr + ni * li) / den)[..., None]
    ci = ((ni * lr - nr * li) / den)[..., None]
    br = b_re.astype(F32)
    bi = b_im.astype(F32)
    bb_re = cr * br - ci * bi
    bb_im = cr * bi + ci * br
    ccr = c_re.astype(F32)
    cci = c_im.astype(F32)

    pr = [jnp.ones_like(ab_re)]
    pi = [jnp.zeros_like(ab_im)]
    for _ in range(c):
        r, i = _cmul(pr[-1], pi[-1], ab_re, ab_im)
        pr.append(r)
        pi.append(i)
    pw_re = jnp.stack(pr)
    pw_im = jnp.stack(pi)

    pb_re, pb_im = _cmul(pw_re[..., None], pw_im[..., None], bb_re[None], bb_im[None])
    cp_re, cp_im = _cmul(ccr[None], cci[None], pw_re[:, :, :, None, :], pw_im[:, :, :, None, :])
    kk = (jnp.einsum('dgop,tdgpi->tdgoi', ccr, pb_re[:c], precision=HIGHEST)
          - jnp.einsum('dgop,tdgpi->tdgoi', cci, pb_im[:c], precision=HIGHEST))

    g = lr.shape[1]
    h = SSM_GROUP
    p = SSM_STATE
    ii = jnp.arange(c)[:, None]
    jj = jnp.arange(c)[None, :]
    lags = jnp.arange(c)[:, None, None]
    shift_f = ((jj - ii)[None] == lags).astype(F32)
    shift_b = ((ii - jj)[None] == lags).astype(F32)
    tt = (jnp.einsum('tij,tgon->ginjo', shift_f, kk[:, 0], precision=HIGHEST)
          + jnp.einsum('tij,tgon->ginjo', shift_b, kk[:, 1], precision=HIGHEST))
    dd = d_skip.astype(F32).reshape(g, h)
    eye_t = jnp.eye(c, dtype=F32)
    eye_h = jnp.eye(h, dtype=F32)
    tt = tt + (eye_t[None, :, None, :, None] * eye_h[None, None, :, None, :]
               * dd[:, None, :, None, None])
    tt = tt.reshape(g, c * h, c * h)

    def s_cols(pbx, d, rev):
        m = pbx[:c, d]
        if rev:
            m = m[::-1]
        return jnp.transpose(m, (1, 0, 3, 2)).reshape(g, c * h, p)

    w1 = jnp.concatenate([
        tt,
        s_cols(pb_re, 0, True), s_cols(pb_re, 1, False),
        s_cols(pb_im, 0, True), s_cols(pb_im, 1, False)], axis=-1)

    def o_rows(cpx, d, sign, fwd):
        m = cpx[1:c + 1, d]
        if not fwd:
            m = m[::-1]
        return sign * jnp.transpose(m, (1, 3, 0, 2)).reshape(g, p, c * h)

    w2 = jnp.concatenate([
        o_rows(cp_re, 0, 1.0, True), o_rows(cp_re, 1, 1.0, False),
        o_rows(cp_im, 0, -1.0, True), o_rows(cp_im, 1, -1.0, False)], axis=1)

    a16 = jnp.zeros((g, 8, 2 * p), F32)
    a16 = a16.at[:, 0, :].set(jnp.concatenate([pw_re[c, 0], pw_re[c, 1]], axis=-1))
    a16 = a16.at[:, 1, :].set(jnp.concatenate([pw_im[c, 0], pw_im[c, 1]], axis=-1))
    return w1.astype(BF16), w2.astype(BF16), a16


def _s5_kernel(x_ref, w1_ref, w2_ref, a_ref, y_ref, z_ref, hin_ref, *, nb, nc_ctx, nc_lat):
    ch = x_ref.shape[-1]
    p = SSM_STATE
    z_ref[...] = _dot(x_ref[0], w1_ref[0])
    a_re = a_ref[0, 0:1, :]
    a_im = a_ref[0, 1:2, :]
    fwd_lane = lax.broadcasted_iota(jnp.int32, (nb, 2 * p), 1) < p

    def phase(base, n, carry):
        def step(i, hc):
            h_re, h_im = hc
            rf = pl.multiple_of((base + i) * nb, nb)
            rb = pl.multiple_of((base + n - 1 - i) * nb, nb)
            hin_ref[pl.ds(rf, nb), 0:p] = h_re[:, 0:p]
            hin_ref[pl.ds(rb, nb), p:2 * p] = h_re[:, p:2 * p]
            hin_ref[pl.ds(rf, nb), 2 * p:3 * p] = h_im[:, 0:p]
            hin_ref[pl.ds(rb, nb), 3 * p:4 * p] = h_im[:, p:2 * p]
            s_re = jnp.where(fwd_lane, z_ref[pl.ds(rf, nb), ch:ch + 2 * p],
                             z_ref[pl.ds(rb, nb), ch:ch + 2 * p])
            s_im = jnp.where(fwd_lane, z_ref[pl.ds(rf, nb), ch + 2 * p:ch + 4 * p],
                             z_ref[pl.ds(rb, nb), ch + 2 * p:ch + 4 * p])
            n_re = a_re * h_re - a_im * h_im + s_re
            n_im = a_re * h_im + a_im * h_re + s_im
            return n_re, n_im
        return lax.fori_loop(0, n, step, carry)

    zero = jnp.zeros((nb, 2 * p), F32)
    hc = phase(0, nc_ctx, (zero, zero))
    phase(nc_ctx, nc_lat, hc)
    y = z_ref[:, 0:ch] + _dot(hin_ref[...].astype(BF16), w2_ref[0])
    y_ref[0] = jax.nn.gelu(y).astype(BF16)


def _regroup_kernel(x_ref, o_ref, *, n_groups):
    h = SSM_GROUP
    for t in range(SSM_CHUNK):
        for g in range(n_groups):
            src = (t * n_groups + g) * h
            o_ref[g, :, t * h:(t + 1) * h] = x_ref[:, src:src + h]


def _ungroup_kernel(y_ref, o_ref, *, n_groups):
    h = SSM_GROUP
    for t in range(SSM_CHUNK):
        for g in range(n_groups):
            dst = (t * n_groups + g) * h
            o_ref[:, dst:dst + h] = y_ref[g, :, t * h:(t + 1) * h]


REGROUP_ROWS = 128


def _regroup(x2, n_groups, inverse):
    c = SSM_CHUNK
    h = SSM_GROUP
    if inverse:
        g, m, _ = x2.shape
    else:
        m = x2.shape[0]
        g = n_groups
    rb = min(REGROUP_ROWS, m)
    assert m % rb == 0
    wide = pl.BlockSpec((rb, c * g * h), lambda i: (i, 0))
    grouped = pl.BlockSpec((g, rb, c * h), lambda i: (0, i, 0))
    kern = functools.partial(_ungroup_kernel if inverse else _regroup_kernel, n_groups=g)
    return pl.pallas_call(
        kern,
        grid=(m // rb,),
        in_specs=[grouped if inverse else wide],
        out_specs=wide if inverse else grouped,
        out_shape=jax.ShapeDtypeStruct((m, c * g * h) if inverse else (g, m, c * h), x2.dtype),
        compiler_params=_cparams(("arbitrary",)),
        name="s5_ungroup" if inverse else "s5_regroup",
    )(x2)


def s5_mixer(u4, w1, w2, a16, layer, lc):
    nc, b, c, dssm = u4.shape
    h = SSM_GROUP
    g = dssm // h
    lt = nc * c
    m = nc * b
    ug = _regroup(u4.reshape(m, c * dssm), g, False)
    kern = functools.partial(_s5_kernel, nb=b, nc_ctx=lc // c, nc_lat=(lt - lc) // c)
    yg = pl.pallas_call(
        kern,
        grid=(g,),
        in_specs=[
            pl.BlockSpec((1, m, c * h), lambda i: (i, 0, 0)),
            pl.BlockSpec((None, 1) + w1.shape[2:], lambda i: (layer, i, 0, 0)),
            pl.BlockSpec((None, 1) + w2.shape[2:], lambda i: (layer, i, 0, 0)),
            pl.BlockSpec((None, 1, 8, 2 * SSM_STATE), lambda i: (layer, i, 0, 0)),
        ],
        out_specs=pl.BlockSpec((1, m, c * h), lambda i: (i, 0, 0)),
        out_shape=jax.ShapeDtypeStruct((g, m, c * h), BF16),
        scratch_shapes=[pltpu.VMEM((m, w1.shape[-1]), F32), pltpu.VMEM((m, 4 * SSM_STATE), F32)],
        compiler_params=_cparams(("arbitrary",)),
        name="s5_mixer",
    )(ug, w1, w2, a16)
    return _regroup(yg, g, True).reshape(nc, b, c, dssm)


def _attn_kernel(lam_ref, q_ref, k_ref, v_ref, g_ref, o_ref, *, n_ctx_tiles, lc, out_scale):
    j = pl.program_id(2)
    lam = lam_ref[0]
    q = q_ref[0]
    lo = lax.broadcasted_iota(jnp.int32, q.shape, 1) < QK_HEAD_DIM
    zq = jnp.zeros_like(q)
    q1 = jnp.where(lo, q, zq)
    q2 = jnp.where(lo, zq, q)
    nt = (((1,), (1,)), ((), ()))

    def run(nk):
        k = k_ref[0, 0:nk, :]
        v = v_ref[0, 0:nk, :]
        s1 = lax.dot_general(q1, k, nt, preferred_element_type=F32)
        s2 = lax.dot_general(q2, k, nt, preferred_element_type=F32)
        e1 = jnp.exp2(s1 - jnp.max(s1, axis=-1, keepdims=True))
        e2 = jnp.exp2(s2 - jnp.max(s2, axis=-1, keepdims=True))
        l1 = jnp.sum(e1, axis=-1, keepdims=True)
        l2 = jnp.sum(e2, axis=-1, keepdims=True)
        a = e1 - e2 * (lam * l1 / l2)
        o = _dot(a.astype(BF16), v) * (1.0 / l1)
        ms = jnp.mean(o * o, axis=-1, keepdims=True)
        o = (o * lax.rsqrt(ms + SUBLN_EPS)) * g_ref[...] * out_scale
        o_ref[0] = o.astype(BF16)

    @pl.when(j < n_ctx_tiles)
    def _():
        run(lc)

    @pl.when(j >= n_ctx_tiles)
    def _():
        run(k_ref.shape[1])


def diff_attention(q, k, v, lam, subln_g, lc, lam_init):
    b, lt, _ = q.shape
    tm = TOKEN_TILE
    nj = lt // tm
    hd = 2 * QK_HEAD_DIM
    kern = functools.partial(_attn_kernel, n_ctx_tiles=lc // tm, lc=lc, out_scale=1.0 - lam_init)
    return pl.pallas_call(
        kern,
        grid=(b, N_HEADS, nj),
        in_specs=[
            pl.BlockSpec(memory_space=pltpu.SMEM),
            pl.BlockSpec((1, tm, hd), lambda i, h, j: (i, j, h)),
            pl.BlockSpec((1, lt, hd), lambda i, h, j: (i, 0, h)),
            pl.BlockSpec((1, lt, V_HEAD_DIM), lambda i, h, j: (i, 0, h)),
            pl.BlockSpec((1, V_HEAD_DIM), lambda i, h, j: (0, 0)),
        ],
        out_specs=pl.BlockSpec((1, tm, V_HEAD_DIM), lambda i, h, j: (i, j, h)),
        out_shape=jax.ShapeDtypeStruct((b, lt, N_HEADS * V_HEAD_DIM), BF16),
        compiler_params=_cparams(("arbitrary", "arbitrary", "arbitrary")),
        name="diff_attention",
    )(lam.reshape(1), q, k, v, subln_g.reshape(1, V_HEAD_DIM))


def _route(logits_t, rb):
    aff = jax.nn.sigmoid(logits_t)
    sel = aff + rb
    s = [sel[e:e + 1] for e in range(N_EXPERTS)]
    npg = EXPERTS_PER_GROUP
    gscore = []
    for g in range(N_EXPERT_GROUPS):
        best = None
        for (i, j) in PAIRS:
            ps = s[g * npg + i] + s[g * npg + j]
            best = ps if best is None else jnp.maximum(best, ps)
        gscore.append(best)
    g_sel = jnp.zeros_like(gscore[0], dtype=jnp.int32)
    g_best = gscore[0]
    for g in range(1, N_EXPERT_GROUPS):
        take = gscore[g] > g_best
        g_sel = jnp.where(take, g, g_sel)
        g_best = jnp.where(take, gscore[g], g_best)

    def pick(rows, i):
        out = rows[i]
        for g in range(1, N_EXPERT_GROUPS):
            out = jnp.where(g_sel == g, rows[g * npg + i], out)
        return out

    xs = [pick(s, i) for i in range(npg)]
    i1 = jnp.zeros_like(g_sel)
    v1 = xs[0]
    for i in range(1, npg):
        take = xs[i] > v1
        i1 = jnp.where(take, i, i1)
        v1 = jnp.where(take, xs[i], v1)
    i2 = jnp.full_like(g_sel, -1)
    v2 = jnp.full_like(v1, -jnp.inf)
    for i in range(npg):
        take = (i1 != i) & ((i2 < 0) | (xs[i] > v2))
        i2 = jnp.where(take, i, i2)
        v2 = jnp.where(take, xs[i], v2)
    lo = jnp.minimum(i1, i2)
    hi = jnp.maximum(i1, i2)
    pair = jnp.zeros_like(lo)
    for pi_, (i, j) in enumerate(PAIRS):
        pair = jnp.where((lo == i) & (hi == j), pi_, pair)
    return g_sel * len(PAIRS) + pair


def _merge_kernel(x_ref, yg_ref, o_ref, gate_ref, mod_ref, g2_ref, wglu_ref, wat_ref, wo_ref,
                  rw_ref, rb_ref, xo_ref, h2_ref, route_ref, cnt_ref):
    d = x_ref.shape[-1]
    tm = x_ref.shape[1]
    m = mod_ref[0]
    yg = jnp.concatenate([yg_ref[ci, 0] for ci in range(yg_ref.shape[0])], axis=0)
    ab = _dot(yg, wglu_ref[...])
    p_s = ab[:, 0:d] * jax.nn.sigmoid(ab[:, d:2 * d])
    p_a = _dot(o_ref[0], wat_ref[...])
    gate = gate_ref[0]
    mix = gate[:, 0:d].astype(F32) * p_s + gate[:, d:2 * d].astype(F32) * p_a
    x = x_ref[0] + m[2:3] * _dot(mix.astype(BF16), wo_ref[...])
    xo_ref[0] = x
    ms = jnp.mean(x * x, axis=-1, keepdims=True)
    y = x * lax.rsqrt(ms + NORM_EPS)
    h2 = (y * g2_ref[...]) * (1.0 + m[4:5]) + m[3:4]
    lanes = 128
    for kk in range(d // lanes):
        h2_ref[pl.ds(kk, tm, stride=d // lanes), :] = h2[:, kk * lanes:(kk + 1) * lanes]
    lt = _router_logits(h2, rw_ref).T[0:N_EXPERTS]
    cls = _route(lt, rb_ref[...])

    @pl.when((pl.program_id(0) == 0) & (pl.program_id(1) == 0))
    def _():
        cnt_ref[...] = jnp.zeros_like(cnt_ref)

    member = lax.broadcasted_iota(jnp.int32, (cnt_ref.shape[0], tm), 0) == cls
    earlier = (lax.broadcasted_iota(jnp.int32, (tm, tm), 0) < lax.broadcasted_iota(jnp.int32, (tm, tm), 1))
    before = _dot(member.astype(BF16), earlier.astype(BF16))
    rank = jnp.sum(jnp.where(member, before + cnt_ref[:, 0:1], 0.0), axis=0, keepdims=True)
    cnt_ref[...] = cnt_ref[...] + jnp.sum(member.astype(F32), axis=1, keepdims=True)
    route_ref[...] = jnp.concatenate([rank.astype(jnp.int32), cls, jnp.zeros((6, tm), jnp.int32)], axis=0)


def merge_and_route(xx, yg, o, gate, mods, g2, wglu, wat, wo, layer, rw_pad, rb, n_ctx_tiles, latent_only=False):
    b, lt, d = xx.shape
    tm = TOKEN_TILE
    off = n_ctx_tiles if latent_only else 0
    nj = lt // tm - off
    t = b * nj * tm
    tok = lambda i, j: (i, j + off, 0)
    const = lambda i, j: (0, 0)

    def mod_idx(i, j):
        return (jnp.where(j + off < n_ctx_tiles, b, i), 0, 0)

    return pl.pallas_call(
        _merge_kernel,
        grid=(b, nj),
        in_specs=[
            pl.BlockSpec((1, tm, d), tok),
            pl.BlockSpec((tm // SSM_CHUNK, 1, SSM_CHUNK, yg.shape[-1]), lambda i, j: (j + off, i, 0, 0)),
            pl.BlockSpec((1, tm, o.shape[-1]), tok),
            pl.BlockSpec((1, tm, gate.shape[-1]), tok),
            pl.BlockSpec((1, 6, d), mod_idx),
            pl.BlockSpec((1, d), const),
            pl.BlockSpec((None,) + wglu.shape[1:], lambda i, j: (layer, 0, 0)),
            pl.BlockSpec((None,) + wat.shape[1:], lambda i, j: (layer, 0, 0)),
            pl.BlockSpec((None,) + wo.shape[1:], lambda i, j: (layer, 0, 0)),
            pl.BlockSpec(rw_pad.shape, const),
            pl.BlockSpec(rb.shape, const),
        ],
        out_specs=[
            pl.BlockSpec((1, tm, d), tok),
            pl.BlockSpec((tm * (d // 128), 128), lambda i, j: (i * nj + j, 0)),
            pl.BlockSpec((8, tm), lambda i, j: (0, i * nj + j)),
            pl.BlockSpec((LANES, LANES), const),
        ],
        out_shape=[
            jax.ShapeDtypeStruct((b, lt, d), F32),
            jax.ShapeDtypeStruct((t * (d // 128), 128), F32),
            jax.ShapeDtypeStruct((8, t), jnp.int32),
            jax.ShapeDtypeStruct((LANES, LANES), F32),
        ],
        input_output_aliases={0: 0},
        compiler_params=_cparams(("arbitrary", "arbitrary")),
        name="merge_and_route",
    )(xx, yg, o, gate, mods, g2.reshape(1, d), wglu, wat, wo, rw_pad, rb)


def block_tables(counts, t):
    r = EXPERT_BLOCK
    ncls = N_CLASSES
    cap = -(-(t + ncls * (r - 1)) // r) * r
    nblk = cap // r
    padded = (counts + r - 1) // r * r
    pend = jnp.cumsum(padded)
    pstart = pend - padded
    n_valid = pend[-1] // r
    blk = jnp.arange(nblk, dtype=jnp.int32)
    blk_cls = jnp.minimum(jnp.sum((pend[None, :] <= (blk * r)[:, None]).astype(jnp.int32), axis=1), ncls - 1)
    last_cls = jnp.max(jnp.where(counts > 0, jnp.arange(ncls, dtype=jnp.int32), 0))
    blk_cls = jnp.where(blk < n_valid, blk_cls, last_cls)
    pair_lo = jnp.array([p[0] for p in PAIRS], jnp.int32)
    pair_hi = jnp.array([p[1] for p in PAIRS], jnp.int32)
    grp = blk_cls // len(PAIRS)
    pr = blk_cls % len(PAIRS)
    e_lo = grp * EXPERTS_PER_GROUP + pair_lo[pr]
    e_hi = grp * EXPERTS_PER_GROUP + pair_hi[pr]
    pstart_pad = jnp.pad(pstart.astype(jnp.int32), (0, 32 - ncls))
    return pstart_pad, e_lo, e_hi, n_valid.astype(jnp.int32).reshape(1), cap


TOKEN_SUBLANES = 8
LANES = 128
GATHER_UNROLL = 8


def _slot_copies(pstart_ref, cls_ref, rank_ref, n, make_copy, sem):
    assert n % GATHER_UNROLL == 0

    def issue(i, carry):
        for uu in range(GATHER_UNROLL):
            r = i * GATHER_UNROLL + uu
            slot = pstart_ref[cls_ref[0, 0, r]] + rank_ref[0, 0, r]
            make_copy(r, slot, sem).start(priority=uu % 2)
        return carry
    lax.fori_loop(0, n // GATHER_UNROLL, issue, 0)


def _tile_rows(ref, first, n):
    ts = TOKEN_SUBLANES
    return ref.at[pl.ds(pl.multiple_of(first * ts, ts), n * ts)]


def _tiles_to_rows(ref, n):
    ts = TOKEN_SUBLANES
    return jnp.concatenate([ref[pl.ds(kk, n, stride=ts), :] for kk in range(ts)], axis=1)


def _dispatch_kernel(pstart_ref, cls_ref, rank_ref, src_ref, zero_ref, dst_ref, sem):
    del zero_ref
    n = cls_ref.shape[-1]

    def make_copy(r, slot, s):
        return pltpu.make_async_copy(_tile_rows(src_ref, r, 1), _tile_rows(dst_ref, slot, 1), s)

    _slot_copies(pstart_ref, cls_ref, rank_ref, n, make_copy, sem)
    pltpu.make_async_copy(src_ref, _tile_rows(dst_ref, 0, n), sem).wait()


def dispatch_tokens(h2, pstart, cls_blk, rank_blk, cap, init):
    ts = TOKEN_SUBLANES
    nblk, _, n = cls_blk.shape
    blk = lambda i, ps: (i, 0, 0)
    grid_spec = pltpu.PrefetchScalarGridSpec(
        num_scalar_prefetch=1,
        grid=(nblk,),
        in_specs=[
            pl.BlockSpec((1, 1, n), blk, memory_space=pltpu.SMEM),
            pl.BlockSpec((1, 1, n), blk, memory_space=pltpu.SMEM),
            pl.BlockSpec((n * ts, LANES), lambda i, ps: (i, 0)),
            pl.BlockSpec(memory_space=pl.ANY),
        ],
        out_specs=pl.BlockSpec(memory_space=pl.ANY),
        scratch_shapes=[pltpu.SemaphoreType.DMA(())],
    )
    return pl.pallas_call(
        _dispatch_kernel,
        grid_spec=grid_spec,
        out_shape=jax.ShapeDtypeStruct((cap * ts, LANES), h2.dtype),
        input_output_aliases={4: 0},
        compiler_params=_cparams(("arbitrary",)),
        name="dispatch_tokens",
    )(pstart, cls_blk, rank_blk, h2, init)


def _combine_kernel(pstart_ref, cls_ref, rank_ref, tab_ref, x_ref, mod_ref, gf_ref, o_ref, buf_ref, sem, *, final):
    n = x_ref.shape[1]

    def make_copy(r, slot, s):
        return pltpu.make_async_copy(_tile_rows(tab_ref, slot, 1), _tile_rows(buf_ref, r, 1), s)

    _slot_copies(pstart_ref, cls_ref, rank_ref, n, make_copy, sem)
    pltpu.make_async_copy(_tile_rows(tab_ref, 0, n), buf_ref, sem).wait()
    g2 = mod_ref[0][5:6]
    x = x_ref[0] + g2 * _tiles_to_rows(buf_ref, n)
    if final:
        ms = jnp.mean(x * x, axis=-1, keepdims=True)
        x = (x * lax.rsqrt(ms + NORM_EPS)) * gf_ref[...]
    o_ref[0] = x


def combine_residual(xx, ys, pstart, cls_blk, rank_blk, mods, n_ctx_tiles, final_g=None):
    b, lt, d = xx.shape
    tm = TOKEN_TILE
    nj = lt // tm
    final = final_g is not None
    off = n_ctx_tiles if final else 0
    tok = lambda i, j, ps: (i, j + off, 0)
    blk = lambda i, j, ps: (i * (nj - off) + j, 0, 0)

    def mod_idx(i, j, ps):
        return (jnp.where(j + off < n_ctx_tiles, b, i), 0, 0)

    grid_spec = pltpu.PrefetchScalarGridSpec(
        num_scalar_prefetch=1,
        grid=(b, nj - off),
        in_specs=[
            pl.BlockSpec((1, 1, tm), blk, memory_space=pltpu.SMEM),
            pl.BlockSpec((1, 1, tm), blk, memory_space=pltpu.SMEM),
            pl.BlockSpec(memory_space=pl.ANY),
            pl.BlockSpec((1, tm, d), tok),
            pl.BlockSpec((1, 6, d), mod_idx),
            pl.BlockSpec((1, d), lambda i, j, ps: (0, 0)),
        ],
        out_specs=pl.BlockSpec((1, tm, d), lambda i, j, ps: (i, j, 0)),
        scratch_shapes=[pltpu.VMEM((tm * TOKEN_SUBLANES, LANES), F32), pltpu.SemaphoreType.DMA(())],
    )
    gf = (final_g if final else jnp.ones((d,), F32)).astype(F32).reshape(1, d)
    return pl.pallas_call(
        functools.partial(_combine_kernel, final=final),
        grid_spec=grid_spec,
        out_shape=jax.ShapeDtypeStruct((b, lt - off * tm, d), F32),
        input_output_aliases={} if final else {4: 0},
        compiler_params=_cparams(("arbitrary", "arbitrary")),
        name="combine_final" if final else "combine_residual",
    )(pstart, cls_blk, rank_blk, ys, xx, mods, gf)


def _router_logits(h2, rw_ref):
    n = h2.shape[0]
    hi = h2.astype(BF16)
    lo = (h2 - hi.astype(F32)).astype(BF16)
    pp = _dot(jnp.concatenate([hi, lo], axis=0), rw_ref[...])
    return (pp[0:n, 0:LANES] + pp[0:n, LANES:2 * LANES]) + (pp[n:2 * n, 0:LANES] + pp[n:2 * n, LANES:2 * LANES])


def _expert_kernel(elo_ref, ehi_ref, nv_ref, x_ref, rw_ref, w1a, w3a, w2a, w1b, w3b, w2b, o_ref):
    i = pl.program_id(0)
    n = x_ref.shape[0] // TOKEN_SUBLANES

    @pl.when(i < nv_ref[0])
    def _():
        x = _tiles_to_rows(x_ref, n)
        xb = x.astype(BF16)

        def ffn(w1, w3, w2):
            h1 = _dot(xb, w1[...])
            h3 = _dot(xb, w3[...])
            hid = (h1 * jax.nn.sigmoid(h1)) * h3
            return _dot(hid.astype(BF16), w2[...])

        aff = jax.nn.sigmoid(_router_logits(x, rw_ref))
        lane = lax.broadcasted_iota(jnp.int32, aff.shape, 1)
        a_lo = jnp.sum(jnp.where(lane == elo_ref[i], aff, 0.0), axis=1, keepdims=True)
        a_hi = jnp.sum(jnp.where(lane == ehi_ref[i], aff, 0.0), axis=1, keepdims=True)
        tot = a_lo + a_hi
        y = (a_lo / tot) * ffn(w1a, w3a, w2a) + (a_hi / tot) * ffn(w1b, w3b, w2b)
        for kk in range(TOKEN_SUBLANES):
            o_ref[pl.ds(kk, n, stride=TOKEN_SUBLANES), :] = y[:, kk * LANES:(kk + 1) * LANES]

    @pl.when(i >= nv_ref[0])
    def _():
        o_ref[...] = jnp.zeros_like(o_ref)


def expert_blocks(xs, rw_pad, e_lo, e_hi, n_valid, w1, w3, w2, layer):
    ts = TOKEN_SUBLANES
    cap = xs.shape[0] // ts
    d = ts * LANES
    r = EXPERT_BLOCK
    nblk = cap // r
    de = w1.shape[-1]
    row = lambda i, elo, ehi, nv: (i, 0)
    wlo = lambda i, elo, ehi, nv: (layer, elo[i], 0, 0)
    whi = lambda i, elo, ehi, nv: (layer, ehi[i], 0, 0)
    grid_spec = pltpu.PrefetchScalarGridSpec(
        num_scalar_prefetch=3,
        grid=(nblk,),
        in_specs=[
            pl.BlockSpec((r * ts, LANES), row),
            pl.BlockSpec(rw_pad.shape, lambda i, elo, ehi, nv: (0, 0)),
            pl.BlockSpec((None, None, d, de), wlo),
            pl.BlockSpec((None, None, d, de), wlo),
            pl.BlockSpec((None, None, de, d), wlo),
            pl.BlockSpec((None, None, d, de), whi),
            pl.BlockSpec((None, None, d, de), whi),
            pl.BlockSpec((None, None, de, d), whi),
        ],
        out_specs=pl.BlockSpec((r * ts, LANES), row),
    )
    return pl.pallas_call(
        _expert_kernel,
        grid_spec=grid_spec,
        out_shape=jax.ShapeDtypeStruct((cap * ts, LANES), F32),
        compiler_params=_cparams(("arbitrary",)),
        name="expert_blocks",
    )(e_lo, e_hi, n_valid, xs, rw_pad, w1, w3, w2, w1, w3, w2)


def kernel(x, c, ctx, c_ctx, ada_w, ada_b, norm1_g, norm2_g, final_g, w_in,
           ssm_lam_re, ssm_lam_im, ssm_log_dt, ssm_b_re, ssm_b_im, ssm_c_re, ssm_c_im, ssm_d, w_glu,
           attn_lambda, attn_subln_g, w_attn_out, w_o, router_w, router_b, exp_w1, exp_w3, exp_w2):
    b, s, d = x.shape
    lc = ctx.shape[1]
    depth = w_in.shape[0]
    tm = TOKEN_TILE
    assert lc % tm == 0 and s % tm == 0 and s % GRID_W == 0 and lc % SSM_CHUNK == 0
    n_ctx_tiles = lc // tm
    lt = lc + s

    xx = (ctx, x)

    n_mod_rows = -(-(b + 1) // 8) * 8
    c_all = jnp.zeros((n_mod_rows, d), F32).at[:b].set(c).at[b].set(c_ctx)
    mods_all = ada_modulation(c_all, ada_w, ada_b).reshape(depth, n_mod_rows, 6, d)

    cos_t, sa_t, sb_t = rope_tables(lc, s)
    assert d == TOKEN_SUBLANES * LANES
    rw_f = jnp.zeros((d, LANES), F32).at[:, :N_EXPERTS].set(router_w.astype(F32))
    rw_hi = rw_f.astype(BF16)
    rw_lo = (rw_f - rw_hi.astype(F32)).astype(BF16)
    rw_pad = jnp.concatenate([rw_hi, rw_lo], axis=1)
    rb = router_b.astype(F32).reshape(N_EXPERTS, 1)

    w_in_bf = w_in.astype(BF16)
    w_glu_bf = w_glu.astype(BF16)
    w_at_bf = w_attn_out.astype(BF16)
    w_o_bf = w_o.astype(BF16)
    e1_bf = exp_w1.astype(BF16)
    e3_bf = exp_w3.astype(BF16)
    e2_bf = exp_w2.astype(BF16)

    w1s, w2s, a16 = jax.vmap(s5_chunk_weights)(ssm_lam_re, ssm_lam_im, ssm_log_dt, ssm_b_re, ssm_b_im,
                                               ssm_c_re, ssm_c_im, ssm_d)

    pending = None
    xs = None
    for l in range(depth):
        lam_init = 0.8 - 0.6 * math.exp(-0.3 * l)
        mods = mods_all[l]
        xx, u, q, k, v, gate = in_projection(xx, mods, norm1_g[l], w_in_bf, l,
                                             cos_t, sa_t, sb_t, n_ctx_tiles, pending)

        yg = s5_mixer(u, w1s, w2s, a16, l, lc)

        lf = attn_lambda[l].astype(F32)
        lam = jnp.exp(jnp.sum(lf[0] * lf[1])) - jnp.exp(jnp.sum(lf[2] * lf[3])) + lam_init
        o = diff_attention(q, k, v, lam, attn_subln_g[l].astype(F32), lc, lam_init)

        xx, h2, ranks, hist = merge_and_route(xx, yg, o, gate, mods, norm2_g[l], w_glu_bf, w_at_bf, w_o_bf, l,
                                              rw_pad, rb, n_ctx_tiles, latent_only=l == depth - 1)

        n_tiles = ranks.shape[1] // tm
        rank_blk = ranks[0].reshape(n_tiles, 1, tm)
        cls_blk = ranks[1].reshape(n_tiles, 1, tm)
        counts = hist[:N_CLASSES, 0].astype(jnp.int32)
        pstart, e_lo, e_hi, n_valid, cap = block_tables(counts, b * lt)
        init = jnp.zeros((cap * TOKEN_SUBLANES, LANES), F32) if xs is None else xs
        xs = dispatch_tokens(h2, pstart, cls_blk, rank_blk, cap, init)
        ys = expert_blocks(xs, rw_pad, e_lo, e_hi, n_valid, e1_bf, e3_bf, e2_bf, l)
        pending = (ys, pstart, cls_blk, rank_blk, mods)

    ys, pstart, cls_blk, rank_blk, mods = pending
    return combine_residual(xx, ys, pstart, cls_blk, rank_blk, mods, n_ctx_tiles, final_g=final_g)
```
